```python
import math
import jax, jax.numpy as jnp
from jax import lax
import numpy as np

D_MODEL = 1024
BATCH = 2
SEQ = 16384
DEPTH = 4
DEC_BATCH = 16
DEC_SEQ = 4096
PAST_LEN = 128

GROUP_W = D_MODEL // 4
N_SUBHEADS = 4
SUBHEAD_DIM = GROUP_W // N_SUBHEADS
IN_COLS = GROUP_W + 2 * GROUP_W + GROUP_W + 2 * GROUP_W
CONV_WIDTH = 31
POOL_WINDOWS = (2, 4, 8, 16)
CHUNK = 128
N_EXPERTS = 32
TOP_K = 4
D_EXPERT = D_MODEL // 2
SWIGLU_LIMIT = 7.0
SWIGLU_ALPHA = 1.702
MOE_BLOCK = 2048
DN_ALPHA = (2.0 * DEPTH) ** 0.25
DN_BETA = (8.0 * DEPTH) ** -0.25
LN_EPS = 1e-5

kernel_name = "hybrid_parallel_fourier_conv_pool_sgu_moe_encoder"


def layer_norm(x, g=None, b=None, eps=LN_EPS):
    xf = x.astype(jnp.float32)
    mu = jnp.mean(xf, axis=-1, keepdims=True)
    var = jnp.mean(jnp.square(xf - mu), axis=-1, keepdims=True)
    y = (xf - mu) * lax.rsqrt(var + eps)
    if g is not None:
        y = y * g.astype(jnp.float32) + b.astype(jnp.float32)
    return y.astype(x.dtype)


def fourier_mixer(a, fnet_w):
    B_, S, _ = a.shape
    ah = a.reshape(B_, S, N_SUBHEADS, SUBHEAD_DIM).astype(jnp.float32)
    f = jnp.fft.fft2(ah, axes=(1, 3), norm="ortho").real.astype(a.dtype)
    f = jnp.einsum("bshc,hcd->bshd", f, fnet_w)
    return f.reshape(B_, S, GROUP_W)


def conformer_conv(val, gate, conv_w, conv_b, conv_ln_g, conv_ln_b, conv_pw):
    g = val * jax.nn.sigmoid(gate)
    kern = conv_w[:, None, :].astype(g.dtype)
    pad = CONV_WIDTH // 2
    g = lax.conv_general_dilated(g, kern, window_strides=(1,), padding=[(pad, pad)],
                                 dimension_numbers=("NWC", "WIO", "NWC"),
                                 feature_group_count=GROUP_W) + conv_b
    g = layer_norm(g, conv_ln_g, conv_ln_b)
    g = jax.nn.silu(g)
    return g @ conv_pw


def pool_mixer(a, pool_w, pool_scale):
    B_, S, _ = a.shape
    af = a.astype(jnp.float32)
    cs = jnp.concatenate([jnp.zeros((B_, 1, GROUP_W), jnp.float32), jnp.cumsum(af, axis=1)], axis=1)
    t = np.arange(S)
    outs = []
    for gi, win in enumerate(POOL_WINDOWS):
        lo = np.maximum(t - win // 2, 0)
        hi = np.minimum(t + win - win // 2, S)
        sl = slice(gi * SUBHEAD_DIM, (gi + 1) * SUBHEAD_DIM)
        csg = cs[..., sl]
        ssum = jnp.take(csg, hi, axis=1) - jnp.take(csg, lo, axis=1)
        cnt = jnp.asarray((hi - lo).astype(np.float32))[None, :, None]
        outs.append(ssum / cnt - af[..., sl])
    p = jnp.stack(outs, axis=2).astype(a.dtype)
    p = jnp.einsum("bsgc,gcd->bsgd", p, pool_w)
    return p.reshape(B_, S, GROUP_W) * pool_scale


def spatial_gating(u, v, sgu_ln_g, sgu_ln_b, sgu_w, sgu_b):
    B_, S, _ = u.shape
    n_chunks = S // CHUNK
    vh = v.reshape(B_, S, N_SUBHEADS, SUBHEAD_DIM)
    vh = layer_norm(vh, sgu_ln_g, sgu_ln_b)
    vh = vh.reshape(B_, n_chunks, CHUNK, N_SUBHEADS, SUBHEAD_DIM)
    mixed = jnp.einsum("hpq,bnqhc->bnphc", sgu_w, vh) + jnp.transpose(sgu_b)[None, None, :, :, None]
    return u * mixed.reshape(B_, S, GROUP_W)


def parallel_mixer(h, w_in, b_in, fnet_w, conv_w, conv_b, conv_ln_g, conv_ln_b, conv_pw,
                   pool_w, pool_scale, sgu_ln_g, sgu_ln_b, sgu_w, sgu_b, w_out, b_out):
    z = h @ w_in + b_in
    G = GROUP_W
    za = z[..., 0:G]
    zb_val, zb_gate = z[..., G:2 * G], z[..., 2 * G:3 * G]
    zc = z[..., 3 * G:4 * G]
    zd_u, zd_v = z[..., 4 * G:5 * G], z[..., 5 * G:6 * G]
    oa = fourier_mixer(za, fnet_w)
    ob = conformer_conv(zb_val, zb_gate, conv_w, conv_b, conv_ln_g, conv_ln_b, conv_pw)
    oc = pool_mixer(zc, pool_w, pool_scale)
    od = spatial_gating(zd_u, zd_v, sgu_ln_g, sgu_ln_b, sgu_w, sgu_b)
    o = jnp.concatenate([oa, ob, oc, od], axis=-1)
    return o @ w_out + b_out


def moe_ffn(h, router_w, router_b, w_gu, b_gu, w_dn, b_dn):
    B_, S, D = h.shape
    N = B_ * S
    t = h.reshape(N, D)
    logits = (t @ router_w).astype(jnp.float32) + router_b.astype(jnp.float32)
    vals, idx = lax.top_k(logits, TOP_K)
    wts = jax.nn.softmax(vals, axis=-1)
    M = N * TOP_K
    blk = min(MOE_BLOCK, max(128, M // N_EXPERTS))
    n_blocks = -(-M // blk) + N_EXPERTS
    P = n_blocks * blk
    flat_e = idx.reshape(M)
    order = jnp.argsort(flat_e)
    sorted_e = flat_e[order]
    tok = (order // TOP_K).astype(jnp.int32)
    counts = jnp.bincount(flat_e, length=N_EXPERTS)
    padded = ((counts + blk - 1) // blk) * blk
    cum_pad = jnp.cumsum(padded)
    start_pad = cum_pad - padded
    start = jnp.cumsum(counts) - counts
    rank = jnp.arange(M, dtype=jnp.int32) - start[sorted_e]
    dest = start_pad[sorted_e] + rank
    buf_tok = jnp.zeros((P,), jnp.int32).at[dest].set(tok)
    buf_w = jnp.zeros((P,), jnp.float32).at[dest].set(wts.reshape(M)[order])
    block_e = jnp.minimum(jnp.searchsorted(cum_pad, jnp.arange(n_blocks) * blk, side="right"),
                          N_EXPERTS - 1).astype(jnp.int32)

    def expert_block(args):
        e, tok_b, w_b = args
        xb = t[tok_b]
        hu = xb @ w_gu[e] + b_gu[e]
        glu = jnp.minimum(hu[:, 0::2], SWIGLU_LIMIT)
        lin = jnp.clip(hu[:, 1::2], -SWIGLU_LIMIT, SWIGLU_LIMIT)
        act = glu * jax.nn.sigmoid(SWIGLU_ALPHA * glu) * (lin + 1.0)
        return (act @ w_dn[e] + b_dn[e]) * w_b[:, None].astype(xb.dtype)

    y_blocks = lax.map(expert_block, (block_e, buf_tok.reshape(n_blocks, blk), buf_w.reshape(n_blocks, blk)))
    y = jax.ops.segment_sum(y_blocks.reshape(P, D), buf_tok, num_segments=N)
    return y.reshape(B_, S, D)


def _trunk(x, c, w_mod, b_mod, w_in, b_in, fnet_w, conv_w, conv_b, conv_ln_g, conv_ln_b, conv_pw,
           pool_w, pool_scale, sgu_ln_g, sgu_ln_b, sgu_w, sgu_b, w_out, b_out, ln1_g, ln1_b,
           router_w, router_b, w_gu, b_gu, w_dn, b_dn, ln2_g, ln2_b):
    D = D_MODEL
    for l in range(DEPTH):
        mod = jax.nn.silu(c) @ w_mod[l] + b_mod[l]
        mod = mod[:, None, :]
        shift1, scale1, gate1 = mod[..., 0:D], mod[..., D:2 * D], mod[..., 2 * D:3 * D]
        shift2, scale2, gate2 = mod[..., 3 * D:4 * D], mod[..., 4 * D:5 * D], mod[..., 5 * D:6 * D]
        h = layer_norm(x) * (1.0 + scale1) + shift1
        m = parallel_mixer(h, w_in[l], b_in[l], fnet_w[l], conv_w[l], conv_b[l], conv_ln_g[l], conv_ln_b[l],
                           conv_pw[l], pool_w[l], pool_scale[l], sgu_ln_g[l], sgu_ln_b[l], sgu_w[l], sgu_b[l],
                           w_out[l], b_out[l])
        x = layer_norm(DN_ALPHA * x + gate1 * m, ln1_g[l], ln1_b[l])
        h2 = layer_norm(x) * (1.0 + scale2) + shift2
        f = moe_ffn(h2, router_w[l], router_b[l], w_gu[l], b_gu[l], w_dn[l], b_dn[l])
        x = layer_norm(DN_ALPHA * x + gate2 * f, ln2_g[l], ln2_b[l])
    return x


def setup_inputs(seed: int = 0) -> dict:
    key = jax.random.key(seed)
    ks = jax.random.split(key, 40)
    f32 = jnp.float32
    L, D, G, H, C = DEPTH, D_MODEL, GROUP_W, N_SUBHEADS, SUBHEAD_DIM

    def nrm(k, shape, scale):
        return jax.random.normal(k, shape, f32) * scale

    gate_offset = jnp.tile(jnp.concatenate([jnp.zeros((2 * D,), f32), jnp.ones((D,), f32)]), 2)
    return {
        "x_prompt": nrm(ks[0], (BATCH, SEQ, D), 1.0),
        "x_sample": nrm(ks[1], (DEC_BATCH, DEC_SEQ, D), 1.0),
        "c_prompt": nrm(ks[2], (BATCH, D), 1.0),
        "c_sample": nrm(ks[3], (DEC_BATCH, D), 1.0),
        "w_mod": nrm(ks[4], (L, D, 6 * D), 0.2 * D ** -0.5),
        "b_mod": nrm(ks[5], (L, 6 * D), 0.02) + gate_offset,
        "w_in": nrm(ks[6], (L, D, IN_COLS), D ** -0.5),
        "b_in": nrm(ks[7], (L, IN_COLS), 0.02),
        "fnet_w": nrm(ks[8], (L, H, C, C), C ** -0.5),
        "conv_w": nrm(ks[9], (L, CONV_WIDTH, G), CONV_WIDTH ** -0.5),
        "conv_b": nrm(ks[10], (L, G), 0.02),
        "conv_ln_g": 1.0 + nrm(ks[11], (L, G), 0.02),
        "conv_ln_b": nrm(ks[12], (L, G), 0.02),
        "conv_pw": nrm(ks[13], (L, G, G), G ** -0.5),
        "pool_w": nrm(ks[14], (L, H, C, C), C ** -0.5),
        "pool_scale": 1.0 + nrm(ks[15], (L, G), 0.02),
        "sgu_ln_g": 1.0 + nrm(ks[16], (L, H, C), 0.02),
        "sgu_ln_b": nrm(ks[17], (L, H, C), 0.02),
        "sgu_w": nrm(ks[18], (L, H, CHUNK, CHUNK), CHUNK ** -0.5),
        "sgu_b": 1.0 + nrm(ks[19], (L, H, CHUNK), 0.02),
        "w_out": nrm(ks[20], (L, D, D), DN_BETA * D ** -0.5),
        "b_out": nrm(ks[21], (L, D), 0.02),
        "ln1_g": 1.0 + nrm(ks[22], (L, D), 0.02),
        "ln1_b": nrm(ks[23], (L, D), 0.02),
        "router_w": nrm(ks[24], (L, D, N_EXPERTS), D ** -0.5),
        "router_b": nrm(ks[25], (L, N_EXPERTS), 0.01),
        "w_gu": nrm(ks[26], (L, N_EXPERTS, D, 2 * D_EXPERT), D ** -0.5),
        "b_gu": nrm(ks[27], (L, N_EXPERTS, 2 * D_EXPERT), 0.02),
        "w_dn": nrm(ks[28], (L, N_EXPERTS, D_EXPERT, D), DN_BETA * D_EXPERT ** -0.5),
        "b_dn": nrm(ks[29], (L, N_EXPERTS, D), 0.02),
        "ln2_g": 1.0 + nrm(ks[30], (L, D), 0.02),
        "ln2_b": nrm(ks[31], (L, D), 0.02),
    }


def reference(x_prompt, x_sample, c_prompt, c_sample, w_mod, b_mod, w_in, b_in, fnet_w, conv_w, conv_b,
              conv_ln_g, conv_ln_b, conv_pw, pool_w, pool_scale, sgu_ln_g, sgu_ln_b, sgu_w, sgu_b,
              w_out, b_out, ln1_g, ln1_b, router_w, router_b, w_gu, b_gu, w_dn, b_dn, ln2_g, ln2_b):
    params = (w_mod, b_mod, w_in, b_in, fnet_w, conv_w, conv_b, conv_ln_g, conv_ln_b, conv_pw,
              pool_w, pool_scale, sgu_ln_g, sgu_ln_b, sgu_w, sgu_b, w_out, b_out, ln1_g, ln1_b,
              router_w, router_b, w_gu, b_gu, w_dn, b_dn, ln2_g, ln2_b)
    y_prompt = _trunk(x_prompt, c_prompt, *params)
    y_sample = _trunk(x_sample, c_sample, *params)
    return (y_prompt, y_sample)
```

```python
import functools
import math

import numpy as np
import jax
import jax.numpy as jnp
from jax import lax
from jax.experimental import pallas as pl
from jax.experimental.pallas import tpu as pltpu

D_MODEL = 1024
DEPTH = 4
GROUP_W = 256
N_SUBHEADS = 4
SUBHEAD_DIM = 64
IN_COLS = 6 * GROUP_W
CONV_WIDTH = 31
POOL_WINDOWS = (2, 4, 8, 16)
CHUNK = 128
N_EXPERTS = 32
TOP_K = 4
D_EXPERT = 512
SWIGLU_LIMIT = 7.0
SWIGLU_ALPHA = 1.702
DN_ALPHA = (2.0 * DEPTH) ** 0.25
LN_EPS = 1e-5

HALO = 16
POOL_TAPS = 16
FFT_ROWS = 8
FFT_COLS = 2048
VMEM_LIMIT = 48 * 1024 * 1024

F32 = jnp.float32
BF16 = jnp.bfloat16


def _tiles(seq):
    t_mix = min(512, seq)
    t_cmb = min(256, seq)
    r_exp = 512 if seq >= 4096 else 256
    return t_mix, t_cmb, r_exp


def _dot(a, b):
    return jnp.dot(a, b, preferred_element_type=F32)


def _split(x):
    hi = x.astype(BF16)
    lo = (x - hi.astype(F32)).astype(BF16)
    return hi, lo


def _dot3(a, b_hi, b_lo):
    a_hi, a_lo = _split(a)
    return _dot(a_hi, b_hi) + _dot(a_lo, b_hi) + _dot(a_hi, b_lo)


def _ln(x):
    mu = jnp.mean(x, axis=-1, keepdims=True)
    d = x - mu
    var = jnp.mean(d * d, axis=-1, keepdims=True)
    return d * lax.rsqrt(var + LN_EPS)


def _params(*sem):
    return pltpu.CompilerParams(dimension_semantics=sem, vmem_limit_bytes=VMEM_LIMIT)


def _mod_kernel(c_ref, w_ref, b_ref, o_ref):
    c = c_ref[...]
    s = c * jax.nn.sigmoid(c)
    w_hi, w_lo = _split(w_ref[0])
    o_ref[0] = _dot3(s, w_hi, w_lo) + b_ref[0]


def _modulation(c_all, w_mod, b_mod):
    bp = c_all.shape[0]
    nl = w_mod.shape[0]
    ncol = w_mod.shape[2] // D_MODEL
    return pl.pallas_call(
        _mod_kernel,
        grid=(nl, ncol),
        in_specs=[pl.BlockSpec((bp, D_MODEL), lambda l, j: (0, 0)),
                  pl.BlockSpec((1, D_MODEL, D_MODEL), lambda l, j: (l, 0, j)),
                  pl.BlockSpec((1, 1, D_MODEL), lambda l, j: (l, 0, j))],
        out_specs=pl.BlockSpec((1, bp, D_MODEL), lambda l, j: (l, 0, j)),
        out_shape=jax.ShapeDtypeStruct((nl, bp, w_mod.shape[2]), F32),
        compiler_params=_params("arbitrary", "arbitrary"),
        name="modulation",
    )(c_all, w_mod, b_mod.reshape(nl, 1, -1))


def _fnet_prep_kernel(cc_ref, sc_ref, w_ref, m1_ref, m2_ref):
    w_hi, w_lo = _split(w_ref[0])
    m1_ref[0] = _dot3(cc_ref[...], w_hi, w_lo).astype(BF16)
    m2_ref[0] = _dot3(sc_ref[...], w_hi, w_lo).astype(BF16)


def _block_diag(w):
    nl, h, c, _ = w.shape
    eye = jnp.eye(h, dtype=w.dtype)
    return jnp.einsum("lhcd,hg->lhcgd", w, eye).reshape(nl, h * c, h * c)


def _fnet_prep(fnet_w):
    nl = fnet_w.shape[0]
    k = np.arange(SUBHEAD_DIM)
    ang = 2.0 * np.pi * np.outer(k, k) / SUBHEAD_DIM
    eye = np.eye(N_SUBHEADS)
    cc = jnp.asarray(np.kron(eye, np.cos(ang)), F32)
    sc = jnp.asarray(np.kron(eye, np.sin(ang)), F32)
    wbd = _block_diag(fnet_w)
    mat = pl.BlockSpec((GROUP_W, GROUP_W), lambda l: (0, 0))
    per_layer = pl.BlockSpec((1, GROUP_W, GROUP_W), lambda l: (l, 0, 0))
    return pl.pallas_call(
        _fnet_prep_kernel,
        grid=(nl,),
        in_specs=[mat, mat, per_layer],
        out_specs=[per_layer, per_layer],
        out_shape=[jax.ShapeDtypeStruct((nl, GROUP_W, GROUP_W), BF16)] * 2,
        compiler_params=_params("arbitrary"),
        name="fnet_prep",
    )(cc, sc, wbd)


def _fft_tables(seq):
    n1 = int(round(math.sqrt(seq)))
    n2 = seq // n1
    assert n1 * n2 == seq and n1 % FFT_ROWS == 0 and (n2 * GROUP_W) % FFT_COLS == 0
    scale = 1.0 / math.sqrt(seq * SUBHEAD_DIM)
    a = np.arange(n1)[:, None, None]
    k2 = np.arange(n2)[None, :, None]
    b = np.arange(n2)[None, None, :]
    ang = 2.0 * np.pi * (((a + n1 * b) * k2) % seq) / seq
    g_re = np.cos(ang) * scale
    g_im = -np.sin(ang) * scale
    k1 = np.arange(n1)
    ang2 = 2.0 * np.pi * (np.outer(k1, k1) % n1) / n1
    to = lambda t: jnp.asarray(t, F32).astype(BF16)
    return n1, n2, to(g_re), to(g_im), to(np.cos(ang2)), to(np.sin(ang2))


def _fft1_kernel(z_ref, gr_ref, gi_ref, vr_ref, vi_ref):
    for j in range(FFT_ROWS):
        xs = z_ref[0, :, j, :].astype(BF16)
        vr_ref[0, j] = _dot(gr_ref[j], xs).astype(BF16)
        vi_ref[0, j] = _dot(gi_ref[j], xs).astype(BF16)


def _fft2_kernel(vr_ref, vi_ref, c_ref, s_ref, m1_ref, m2_ref, o_ref):
    vr = vr_ref[0]
    vi = vi_ref[0]
    c = c_ref[...]
    s = s_ref[...]
    yr = _dot(c, vr) + _dot(s, vi)
    yi = _dot(c, vi) - _dot(s, vr)
    m1 = m1_ref[0]
    m2 = m2_ref[0]
    for g in range(FFT_COLS // GROUP_W):
        sl = slice(g * GROUP_W, (g + 1) * GROUP_W)
        o_ref[0, :, sl] = (_dot(yr[:, sl].astype(BF16), m1) + _dot(yi[:, sl].astype(BF16), m2)).astype(BF16)


def _fourier(z, batch, seq, tables, m1, m2, layer):
    n1, n2, g_re, g_im, c2, s2 = tables
    z4 = z.reshape(batch, n2, n1, IN_COLS)
    v_shape = jax.ShapeDtypeStruct((batch, n1, n2, GROUP_W), BF16)
    g_spec = pl.BlockSpec((FFT_ROWS, n2, n2), lambda b, i: (i, 0, 0))
    v_spec = pl.BlockSpec((1, FFT_ROWS, n2, GROUP_W), lambda b, i: (b, i, 0, 0))
    vr, vi = pl.pallas_call(
        _fft1_kernel,
        grid=(batch, n1 // FFT_ROWS),
        in_specs=[pl.BlockSpec((1, n2, FFT_ROWS, GROUP_W), lambda b, i: (b, 0, i, 0)), g_spec, g_spec],
        out_specs=[v_spec, v_spec],
        out_shape=[v_shape, v_shape],
        compiler_params=_params("arbitrary", "arbitrary"),
        name="fft_stage1",
    )(z4, g_re, g_im)
    width = n2 * GROUP_W
    vr = vr.reshape(batch, n1, width)
    vi = vi.reshape(batch, n1, width)
    blk = pl.BlockSpec((1, n1, FFT_COLS), lambda b, j: (b, 0, j))
    tab = pl.BlockSpec((n1, n1), lambda b, j: (0, 0))
    mat = pl.BlockSpec((1, GROUP_W, GROUP_W), lambda b, j: (layer, 0, 0))
    oa = pl.pallas_call(
        _fft2_kernel,
        grid=(batch, width // FFT_COLS),
        in_specs=[blk, blk, tab, tab, mat, mat],
        out_specs=blk,
        out_shape=jax.ShapeDtypeStruct((batch, n1, width), BF16),
        compiler_params=_params("arbitrary", "arbitrary"),
        name="fft_stage2",
    )(vr, vi, c2, s2, m1, m2)
    return oa.reshape(batch, seq, GROUP_W)


def _inproj_kernel(x_ref, mod_ref, w_ref, b_ref, z_ref):
    m = mod_ref[0]
    h = _ln(x_ref[0]) * (1.0 + m[1:2]) + m[0:1]
    z_ref[0] = _dot(h.astype(BF16), w_ref[0]) + b_ref[0]


def _inproj(x, mod, w_in, b_in, layer, tile):
    batch, seq, _ = x.shape
    return pl.pallas_call(
        _inproj_kernel,
        grid=(batch, seq // tile),
        in_specs=[pl.BlockSpec((1, tile, D_MODEL), lambda b, i: (b, i, 0)),
                  pl.BlockSpec((1, 6, D_MODEL), lambda b, i: (b, 0, 0)),
                  pl.BlockSpec((1, D_MODEL, IN_COLS), lambda b, i: (layer, 0, 0)),
                  pl.BlockSpec((1, 1, IN_COLS), lambda b, i: (layer, 0, 0))],
        out_specs=pl.BlockSpec((1, tile, IN_COLS), lambda b, i: (b, i, 0)),
        out_shape=jax.ShapeDtypeStruct((batch, seq, IN_COLS), F32),
        compiler_params=_params("arbitrary", "arbitrary"),
        name="in_proj",
    )(x, mod, w_in, b_in)


def _mix_kernel(seq, tile,
                x_ref, oa_ref,
                bvp_ref, bvm_ref, bvn_ref, bgp_ref, bgm_ref, bgn_ref,
                pcp_ref, pcm_ref, pcn_ref, du_ref, dv_ref, mod_ref,
                convw_ref, convb_ref, clng_ref, clnb_ref, cpw_ref,
                pmask_ref, poolw_ref, pscale_ref,
                slng_ref, slnb_ref, wst_ref, sbias_ref, avg_ref,
                wout_ref, bout_ref, ln1g_ref, ln1b_ref,
                rwh_ref, rwl_ref, rb_ref, tri_ref,
                x1_ref, h2_ref, idx_ref, wts_ref, lrank_ref, cnt_ref,
                gbuf, pbuf):
    i = pl.program_id(1)
    first = i == 0
    last = i == pl.num_programs(1) - 1
    m = mod_ref[0]

    def glu(v_ref, g_ref):
        return v_ref[0] * jax.nn.sigmoid(g_ref[0])

    gbuf[0:HALO] = jnp.where(first, 0.0, glu(bvp_ref, bgp_ref))
    gbuf[HALO:HALO + tile] = glu(bvm_ref, bgm_ref)
    gbuf[HALO + tile:] = jnp.where(last, 0.0, glu(bvn_ref, bgn_ref))
    acc = jnp.zeros((tile, GROUP_W), F32)
    off = HALO - CONV_WIDTH // 2
    for j in range(CONV_WIDTH):
        acc = acc + gbuf[off + j:off + j + tile, :] * convw_ref[0, j:j + 1, :]
    cv = _ln(acc + convb_ref[0]) * clng_ref[0] + clnb_ref[0]
    cv = cv * jax.nn.sigmoid(cv)
    ob = _dot(cv.astype(BF16), cpw_ref[0])

    pc = pcm_ref[0]
    pbuf[0:HALO] = jnp.where(first, 0.0, pcp_ref[0])
    pbuf[HALO:HALO + tile] = pc
    pbuf[HALO + tile:] = jnp.where(last, 0.0, pcn_ref[0])
    ssum = jnp.zeros((tile, GROUP_W), F32)
    off = HALO - POOL_TAPS // 2
    for j in range(POOL_TAPS):
        ssum = ssum + pbuf[off + j:off + j + tile, :] * pmask_ref[j:j + 1, :]
    lane = lax.broadcasted_iota(jnp.int32, (tile, GROUP_W), 1)
    half = jnp.left_shift(1, lane // SUBHEAD_DIM)
    pos = i * tile + lax.broadcasted_iota(jnp.int32, (tile, GROUP_W), 0)
    cnt = jnp.minimum(pos + half, seq) - jnp.maximum(pos - half, 0)
    pooled = ssum / cnt.astype(F32) - pc
    oc = _dot(pooled.astype(BF16), poolw_ref[0]) * pscale_ref[0]

    v = dv_ref[0]
    avg = avg_ref[...]
    v_hi, v_lo = _split(v)
    dv = v - (_dot(v_hi, avg) + _dot(v_lo, avg))
    q_hi, q_lo = _split(dv * dv)
    var = _dot(q_hi, avg) + _dot(q_lo, avg)
    vn = (dv * lax.rsqrt(var + LN_EPS) * slng_ref[0] + slnb_ref[0]).astype(BF16)
    wst = wst_ref[0]
    lane_c = lax.broadcasted_iota(jnp.int32, (CHUNK, GROUP_W), 1) // SUBHEAD_DIM
    mixed = []
    for c in range(tile // CHUNK):
        res = _dot(wst, vn[c * CHUNK:(c + 1) * CHUNK])
        sel = res[0:CHUNK]
        for h in range(1, N_SUBHEADS):
            sel = jnp.where(lane_c == h, res[h * CHUNK:(h + 1) * CHUNK], sel)
        mixed.append(sel + sbias_ref[0])
    od = du_ref[0] * jnp.concatenate(mixed, axis=0)

    g = GROUP_W
    mo = (_dot(oa_ref[0], wout_ref[0, 0:g]) + _dot(ob.astype(BF16), wout_ref[0, g:2 * g])
          + _dot(oc.astype(BF16), wout_ref[0, 2 * g:3 * g]) + _dot(od.astype(BF16), wout_ref[0, 3 * g:4 * g])
          + bout_ref[0])
    x1 = _ln(DN_ALPHA * x_ref[0] + m[2:3] * mo) * ln1g_ref[0] + ln1b_ref[0]
    x1_ref[0] = x1

    h2 = _ln(x1) * (1.0 + m[4:5]) + m[3:4]
    h2_ref[0] = h2
    logits = _dot3(h2, rwh_ref[0], rwl_ref[0]) + rb_ref[0]
    eio = lax.broadcasted_iota(jnp.int32, (tile, N_EXPERTS), 1)
    kio = lax.broadcasted_iota(jnp.int32, (tile, TOP_K), 1)
    work = logits
    picks, vals, hots = [], [], []
    for _ in range(TOP_K):
        mx = jnp.max(work, axis=1, keepdims=True)
        am = jnp.min(jnp.where(work == mx, eio, N_EXPERTS), axis=1, keepdims=True)
        hot = eio == am
        picks.append(am)
        vals.append(mx)
        hots.append(hot)
        work = jnp.where(hot, -jnp.inf, work)
    exps = [jnp.exp(vv - vals[0]) for vv in vals]
    denom = exps[0] + exps[1] + exps[2] + exps[3]
    onehot = jnp.zeros((tile, N_EXPERTS), F32)
    for hot in hots:
        onehot = onehot + hot.astype(F32)
    before = _dot(tri_ref[...], onehot.astype(BF16))
    idx = jnp.zeros((tile, TOP_K), jnp.int32)
    wts = jnp.zeros((tile, TOP_K), F32)
    lrank = jnp.zeros((tile, TOP_K), jnp.int32)
    for k in range(TOP_K):
        rk = jnp.sum(jnp.where(hots[k], before, 0.0), axis=1, keepdims=True).astype(jnp.int32)
        idx = jnp.where(kio == k, picks[k], idx)
        wts = jnp.where(kio == k, exps[k] / denom, wts)
        lrank = jnp.where(kio == k, rk, lrank)
    idx_ref[0] = idx
    wts_ref[0] = wts
    lrank_ref[0] = lrank
    cnt_ref[0, 0] = jnp.sum(onehot, axis=0, keepdims=True).astype(jnp.int32)


def _mix(x, z, oa, mod, p, layer, tile):
    batch, seq, _ = x.shape
    nt = seq // tile
    hb = tile // HALO
    nhb = seq // HALO

    def col(c):
        return pl.BlockSpec((1, tile, GROUP_W), lambda b, i: (b, i, c))

    def prev(c):
        return pl.BlockSpec((1, HALO, GROUP_W), lambda b, i: (b, jnp.maximum(i * hb - 1, 0), c))

    def nxt(c):
        return pl.BlockSpec((1, HALO, GROUP_W), lambda b, i: (b, jnp.minimum((i + 1) * hb, nhb - 1), c))

    def lay(*shape):
        return pl.BlockSpec((1,) + shape, lambda b, i: (layer,) + (0,) * len(shape))

    def const(*shape):
        return pl.BlockSpec(shape, lambda b, i: (0,) * len(shape))

    row = pl.BlockSpec((1, tile, D_MODEL), lambda b, i: (b, i, 0))
    sel = pl.BlockSpec((1, tile, TOP_K), lambda b, i: (b, i, 0))
    in_specs = [
        row, col(0),
        prev(1), col(1), nxt(1), prev(2), col(2), nxt(2),
        prev(3), col(3), nxt(3), col(4), col(5),
        pl.BlockSpec((1, 6, D_MODEL), lambda b, i: (b, 0, 0)),
        lay(CONV_WIDTH, GROUP_W), lay(1, GROUP_W), lay(1, GROUP_W), lay(1, GROUP_W), lay(GROUP_W, GROUP_W),
        const(POOL_TAPS, GROUP_W), lay(GROUP_W, GROUP_W), lay(1, GROUP_W),
        lay(1, GROUP_W), lay(1, GROUP_W), lay(N_SUBHEADS * CHUNK, CHUNK), lay(CHUNK, GROUP_W),
        const(GROUP_W, GROUP_W),
        lay(D_MODEL, D_MODEL), lay(1, D_MODEL), lay(1, D_MODEL), lay(1, D_MODEL),
        lay(D_MODEL, N_EXPERTS), lay(D_MODEL, N_EXPERTS), lay(1, N_EXPERTS), const(tile, tile),
    ]
    out_specs = [row, row, sel, sel, sel,
                 pl.BlockSpec((1, 1, 1, N_EXPERTS), lambda b, i: (b, i, 0, 0))]
    out_shape = [jax.ShapeDtypeStruct((batch, seq, D_MODEL), F32),
                 jax.ShapeDtypeStruct((batch, seq, D_MODEL), F32),
                 jax.ShapeDtypeStruct((batch, seq, TOP_K), jnp.int32),
                 jax.ShapeDtypeStruct((batch, seq, TOP_K), F32),
                 jax.ShapeDtypeStruct((batch, seq, TOP_K), jnp.int32),
                 jax.ShapeDtypeStruct((batch, nt, 1, N_EXPERTS), jnp.int32)]
    tri = jnp.asarray(np.tril(np.ones((tile, tile), np.float32), -1), BF16)
    return pl.pallas_call(
        functools.partial(_mix_kernel, seq, tile),
        grid=(batch, nt),
        in_specs=in_specs,
        out_specs=out_specs,
        out_shape=out_shape,
        scratch_shapes=[pltpu.VMEM((tile + 2 * HALO, GROUP_W), F32),
                        pltpu.VMEM((tile + 2 * HALO, GROUP_W), F32)],
        compiler_params=_params("arbitrary", "arbitrary"),
        name="mix",
    )(x, oa, z, z, z, z, z, z, z, z, z, z, z, mod,
      p["conv_w"], p["conv_b"], p["conv_ln_g"], p["conv_ln_b"], p["conv_pw"],
      p["pool_mask"], p["pool_w"], p["pool_scale"],
      p["sgu_ln_g"], p["sgu_ln_b"], p["sgu_w"], p["sgu_b"], p["avg"],
      p["w_out"], p["b_out"], p["ln1_g"], p["ln1_b"],
      p["router_hi"], p["router_lo"], p["router_b"], tri)


def _route(idx, lrank, cnt, tile, rows):
    n = idx.shape[0] * idx.shape[1]
    idx = idx.reshape(n, TOP_K)
    lrank = lrank.reshape(n, TOP_K)
    cnt = cnt.reshape(-1, N_EXPERTS)
    total = jnp.sum(cnt, axis=0)
    padded = ((total + rows - 1) // rows) * rows
    cum_pad = jnp.cumsum(padded)
    base = (cum_pad - padded)[None, :] + jnp.cumsum(cnt, axis=0) - cnt
    base_tok = jnp.repeat(base, tile, axis=0)
    dest = jnp.take_along_axis(base_tok, idx, axis=1) + lrank
    n_blocks = n * TOP_K // rows + N_EXPERTS
    block_e = jnp.minimum(jnp.searchsorted(cum_pad, jnp.arange(n_blocks) * rows, side="right"),
                          N_EXPERTS - 1).astype(jnp.int32)
    return dest.reshape(-1).astype(jnp.int32), block_e, n_blocks


def _row_copy(src, src_row, dst, dst_row, sem):
    return pltpu.make_async_copy(src.at[pl.ds(src_row, 1)], dst.at[pl.ds(dst_row, 1)], sem)


def _dispatch_kernel(tile, dest_ref, h2_hbm, xs_in, xs_hbm, sem):
    del xs_in
    base = pl.program_id(0) * tile

    def body(n, carry):
        for k in range(TOP_K):
            _row_copy(h2_hbm, base + n, xs_hbm, dest_ref[n * TOP_K + k], sem).start()
        return carry

    lax.fori_loop(0, tile, body, 0)
    pltpu.make_async_copy(h2_hbm.at[pl.ds(0, tile * TOP_K)], xs_hbm.at[pl.ds(0, tile * TOP_K)], sem).wait()


def _dispatch(dest, h2, total_rows, tile):
    n = h2.shape[0]
    xs0 = jnp.zeros((total_rows, D_MODEL), F32)
    return pl.pallas_call(
        functools.partial(_dispatch_kernel, tile),
        grid=(n // tile,),
        in_specs=[pl.BlockSpec((tile * TOP_K,), lambda i: (i,), memory_space=pltpu.SMEM),
                  pl.BlockSpec(memory_space=pl.ANY),
                  pl.BlockSpec(memory_space=pl.ANY)],
        out_specs=pl.BlockSpec(memory_space=pl.ANY),
        out_shape=jax.ShapeDtypeStruct((total_rows, D_MODEL), F32),
        scratch_shapes=[pltpu.SemaphoreType.DMA],
        input_output_aliases={2: 0},
        compiler_params=_params("arbitrary"),
        name="dispatch",
    )(dest, h2, xs0)


def _expert_kernel(be_ref, xs_ref, wg_ref, wl_ref, bg_ref, bl_ref, wd_ref, bd_ref, y_ref):
    del be_ref
    xb = xs_ref[...].astype(BF16)
    glu = jnp.minimum(_dot(xb, wg_ref[0, 0]) + bg_ref[0, 0], SWIGLU_LIMIT)
    lin = jnp.clip(_dot(xb, wl_ref[0, 0]) + bl_ref[0, 0], -SWIGLU_LIMIT, SWIGLU_LIMIT)
    act = glu * jax.nn.sigmoid(SWIGLU_ALPHA * glu) * (lin + 1.0)
    y_ref[...] = _dot(act.astype(BF16), wd_ref[0, 0]) + bd_ref[0, 0]


def _experts(block_e, xs, p, layer, rows, n_blocks):
    def wspec(*shape):
        return pl.BlockSpec((1, 1) + shape, lambda g, be: (layer, be[g]) + (0,) * len(shape))

    grid_spec = pltpu.PrefetchScalarGridSpec(
        num_scalar_prefetch=1,
        grid=(n_blocks,),
        in_specs=[pl.BlockSpec((rows, D_MODEL), lambda g, be: (g, 0)),
                  wspec(D_MODEL, D_EXPERT), wspec(D_MODEL, D_EXPERT),
                  wspec(1, D_EXPERT), wspec(1, D_EXPERT),
                  wspec(D_EXPERT, D_MODEL), wspec(1, D_MODEL)],
        out_specs=pl.BlockSpec((rows, D_MODEL), lambda g, be: (g, 0)),
    )
    return pl.pallas_call(
        _expert_kernel,
        grid_spec=grid_spec,
        out_shape=jax.ShapeDtypeStruct(xs.shape, F32),
        compiler_params=_params("arbitrary"),
        name="experts",
    )(block_e, xs, p["w_glu"], p["w_lin"], p["b_glu"], p["b_lin"], p["w_dn"], p["b_dn"])


def _combine_kernel(tile, dest_ref, y_hbm, x1_ref, wts_ref, mod_ref, g_ref, b_ref, o_ref, buf, sem):
    def body(n, carry):
        for k in range(TOP_K):
            _row_copy(y_hbm, dest_ref[n * TOP_K + k], buf, k * tile + n, sem).start()
        return carry

    lax.fori_loop(0, tile, body, 0)
    pltpu.make_async_copy(y_hbm.at[pl.ds(0, tile * TOP_K)], buf, sem).wait()
    wts = wts_ref[0]
    f = wts[:, 0:1] * buf[0:tile]
    for k in range(1, TOP_K):
        f = f + wts[:, k:k + 1] * buf[k * tile:(k + 1) * tile]
    m = mod_ref[0]
    o_ref[0] = _ln(DN_ALPHA * x1_ref[0] + m[5:6] * f) * g_ref[0] + b_ref[0]


def _combine(dest, y, x1, wts, mod, p, layer, tile):
    batch, seq, _ = x1.shape
    nt = seq // tile
    row = pl.BlockSpec((1, tile, D_MODEL), lambda b, i: (b, i, 0))
    vec = pl.BlockSpec((1, 1, D_MODEL), lambda b, i: (layer, 0, 0))
    return pl.pallas_call(
        functools.partial(_combine_kernel, tile),
        grid=(batch, nt),
        in_specs=[pl.BlockSpec((tile * TOP_K,), lambda b, i: (b * nt + i,), memory_space=pltpu.SMEM),
                  pl.BlockSpec(memory_space=pl.ANY),
                  row,
                  pl.BlockSpec((1, tile, TOP_K), lambda b, i: (b, i, 0)),
                  pl.BlockSpec((1, 6, D_MODEL), lambda b, i: (b, 0, 0)),
                  vec, vec],
        out_specs=row,
        out_shape=jax.ShapeDtypeStruct(x1.shape, F32),
        scratch_shapes=[pltpu.VMEM((tile * TOP_K, D_MODEL), F32), pltpu.SemaphoreType.DMA],
        compiler_params=_params("arbitrary", "arbitrary"),
        name="combine",
    )(dest, y, x1, wts, mod, p["ln2_g"], p["ln2_b"])


def _prepare(w_in, b_in, fnet_w, conv_w, conv_b, conv_ln_g, conv_ln_b, conv_pw, pool_w, pool_scale,
             sgu_ln_g, sgu_ln_b, sgu_w, sgu_b, w_out, b_out, ln1_g, ln1_b, router_w, router_b,
             w_gu, b_gu, w_dn, b_dn, ln2_g, ln2_b):
    nl = w_in.shape[0]
    vec = lambda t: t.reshape(nl, 1, -1)
    half = np.repeat(np.array(POOL_WINDOWS) // 2, SUBHEAD_DIM)[None, :]
    tap = np.arange(POOL_TAPS)[:, None] - POOL_TAPS // 2
    pool_mask = ((tap >= -half) & (tap < half)).astype(np.float32)
    group = np.arange(GROUP_W) // SUBHEAD_DIM
    avg = (group[:, None] == group[None, :]).astype(np.float32) / SUBHEAD_DIM
    r_hi = router_w.astype(BF16)
    m1, m2 = _fnet_prep(fnet_w)
    return {
        "w_in": w_in.astype(BF16), "b_in": vec(b_in), "fnet_m1": m1, "fnet_m2": m2,
        "conv_w": conv_w, "conv_b": vec(conv_b), "conv_ln_g": vec(conv_ln_g), "conv_ln_b": vec(conv_ln_b),
        "conv_pw": conv_pw.astype(BF16),
        "pool_mask": jnp.asarray(pool_mask), "pool_w": _block_diag(pool_w).astype(BF16),
        "pool_scale": vec(pool_scale),
        "sgu_ln_g": vec(sgu_ln_g), "sgu_ln_b": vec(sgu_ln_b),
        "sgu_w": sgu_w.reshape(nl, N_SUBHEADS * CHUNK, CHUNK).astype(BF16),
        "sgu_b": jnp.repeat(jnp.swapaxes(sgu_b, 1, 2), SUBHEAD_DIM, axis=2),
        "avg": jnp.asarray(avg, BF16),
        "w_out": w_out.astype(BF16), "b_out": vec(b_out), "ln1_g": vec(ln1_g), "ln1_b": vec(ln1_b),
        "router_hi": r_hi, "router_lo": (router_w - r_hi.astype(F32)).astype(BF16), "router_b": vec(router_b),
        "w_glu": w_gu[..., 0::2].astype(BF16), "w_lin": w_gu[..., 1::2].astype(BF16),
        "b_glu": b_gu[..., None, 0::2], "b_lin": b_gu[..., None, 1::2],
        "w_dn": w_dn.astype(BF16), "b_dn": b_dn[..., None, :],
        "ln2_g": vec(ln2_g), "ln2_b": vec(ln2_b),
    }


def _trunk(x, mod, p):
    batch, seq, _ = x.shape
    t_mix, t_cmb, rows = _tiles(seq)
    tables = _fft_tables(seq)
    for layer in range(DEPTH):
        z = _inproj(x, mod[layer], p["w_in"], p["b_in"], layer, t_mix)
        oa = _fourier(z, batch, seq, tables, p["fnet_m1"], p["fnet_m2"], layer)
        x1, h2, idx, wts, lrank, cnt = _mix(x, z, oa, mod[layer], p, layer, t_mix)
        dest, block_e, n_blocks = _route(idx, lrank, cnt, t_mix, rows)
        xs = _dispatch(dest, h2.reshape(batch * seq, D_MODEL), n_blocks * rows, t_mix)
        y = _experts(block_e, xs, p, layer, rows, n_blocks)
        x = _combine(dest, y, x1, wts, mod[layer], p, layer, t_cmb)
    return x


def kernel(x_prompt, x_sample, c_prompt, c_sample, w_mod, b_mod, w_in, b_in, fnet_w, conv_w, conv_b,
           conv_ln_g, conv_ln_b, conv_pw, pool_w, pool_scale, sgu_ln_g, sgu_ln_b, sgu_w, sgu_b,
           w_out, b_out, ln1_g, ln1_b, router_w, router_b, w_gu, b_gu, w_dn, b_dn, ln2_g, ln2_b):
    p = _prepare(w_in, b_in, fnet_w, conv_w, conv_b, conv_ln_g, conv_ln_b, conv_pw, pool_w, pool_scale,
                 sgu_ln_g, sgu_ln_b, sgu_w, sgu_b, w_out, b_out, ln1_g, ln1_b, router_w, router_b,
                 w_gu, b_gu, w_dn, b_dn, ln2_g, ln2_b)
    nb_p, nb_s = c_prompt.shape[0], c_sample.shape[0]
    pad = (-(nb_p + nb_s)) % 8
    c_all = jnp.concatenate([c_prompt, c_sample, jnp.zeros((pad, D_MODEL), F32)], axis=0)
    mod = _modulation(c_all, w_mod, b_mod)
    nl = mod.shape[0]
    mod_p = mod[:, :nb_p].reshape(nl, nb_p, 6, D_MODEL)
    mod_s = mod[:, nb_p:nb_p + nb_s].reshape(nl, nb_s, 6, D_MODEL)
    return _trunk(x_prompt, mod_p, p), _trunk(x_sample, mod_s, p)
```

```python
import functools
import math

import numpy as np
import jax
import jax.numpy as jnp
from jax import lax
from jax.experimental import pallas as pl
from jax.experimental.pallas import tpu as pltpu

D_MODEL = 1024
DEPTH = 4
GROUP_W = 256
N_SUBHEADS = 4
SUBHEAD_DIM = 64
IN_COLS = 6 * GROUP_W
CONV_WIDTH = 31
POOL_WINDOWS = (2, 4, 8, 16)
CHUNK = 128
N_EXPERTS = 32
TOP_K = 4
D_EXPERT = 512
SWIGLU_LIMIT = 7.0
SWIGLU_ALPHA = 1.702
DN_ALPHA = (2.0 * DEPTH) ** 0.25
LN_EPS = 1e-5

HALO = 16
POOL_TAPS = 16
FFT_ROWS = 8
FFT_COLS = 2048
DMA_UNROLL = 8
VMEM_LIMIT = 48 * 1024 * 1024

F32 = jnp.float32
BF16 = jnp.bfloat16

_GU_ORDER = np.concatenate([np.arange(0, 2 * D_EXPERT, 2), np.arange(1, 2 * D_EXPERT, 2)])


def _tiles(seq):
    t_mix = min(512, seq)
    t_cmb = min(256, seq)
    r_exp = 512 if seq >= 4096 else 256
    return t_mix, t_cmb, r_exp


def _dot(a, b):
    return jnp.dot(a, b, preferred_element_type=F32)


def _split(x):
    hi = x.astype(BF16)
    lo = (x - hi.astype(F32)).astype(BF16)
    return hi, lo


def _dot3(a, b_hi, b_lo):
    a_hi, a_lo = _split(a)
    return _dot(a_hi, b_hi) + _dot(a_lo, b_hi) + _dot(a_hi, b_lo)


def _ln(x):
    mu = jnp.mean(x, axis=-1, keepdims=True)
    d = x - mu
    var = jnp.mean(d * d, axis=-1, keepdims=True)
    return d * lax.rsqrt(var + LN_EPS)


def _params(*sem):
    return pltpu.CompilerParams(dimension_semantics=sem, vmem_limit_bytes=VMEM_LIMIT)


def _mod_kernel(c_ref, w_ref, b_ref, o_ref):
    c = c_ref[...]
    s = c * jax.nn.sigmoid(c)
    w_hi, w_lo = _split(w_ref[0])
    o_ref[0] = _dot3(s, w_hi, w_lo) + b_ref[0]


def _modulation(c_all, w_mod, b_mod):
    bp = c_all.shape[0]
    nl = w_mod.shape[0]
    ncol = w_mod.shape[2] // D_MODEL
    return pl.pallas_call(
        _mod_kernel,
        grid=(nl, ncol),
        in_specs=[pl.BlockSpec((bp, D_MODEL), lambda l, j: (0, 0)),
                  pl.BlockSpec((1, D_MODEL, D_MODEL), lambda l, j: (l, 0, j)),
                  pl.BlockSpec((1, 1, D_MODEL), lambda l, j: (l, 0, j))],
        out_specs=pl.BlockSpec((1, bp, D_MODEL), lambda l, j: (l, 0, j)),
        out_shape=jax.ShapeDtypeStruct((nl, bp, w_mod.shape[2]), F32),
        compiler_params=_params("arbitrary", "arbitrary"),
        name="modulation",
    )(c_all, w_mod, b_mod.reshape(nl, 1, -1))


def _fnet_prep_kernel(cc_ref, sc_ref, w_ref, m1_ref, m2_ref):
    w_hi, w_lo = _split(w_ref[0])
    m1_ref[0] = _dot3(cc_ref[...], w_hi, w_lo).astype(BF16)
    m2_ref[0] = _dot3(sc_ref[...], w_hi, w_lo).astype(BF16)


def _block_diag(w):
    nl, h, c, _ = w.shape
    eye = jnp.eye(h, dtype=w.dtype)
    return jnp.einsum("lhcd,hg->lhcgd", w, eye).reshape(nl, h * c, h * c)


def _fnet_prep(fnet_w):
    nl = fnet_w.shape[0]
    k = np.arange(SUBHEAD_DIM)
    ang = 2.0 * np.pi * np.outer(k, k) / SUBHEAD_DIM
    eye = np.eye(N_SUBHEADS)
    cc = jnp.asarray(np.kron(eye, np.cos(ang)), F32)
    sc = jnp.asarray(np.kron(eye, np.sin(ang)), F32)
    wbd = _block_diag(fnet_w)
    mat = pl.BlockSpec((GROUP_W, GROUP_W), lambda l: (0, 0))
    per_layer = pl.BlockSpec((1, GROUP_W, GROUP_W), lambda l: (l, 0, 0))
    return pl.pallas_call(
        _fnet_prep_kernel,
        grid=(nl,),
        in_specs=[mat, mat, per_layer],
        out_specs=[per_layer, per_layer],
        out_shape=[jax.ShapeDtypeStruct((nl, GROUP_W, GROUP_W), BF16)] * 2,
        compiler_params=_params("arbitrary"),
        name="fnet_prep",
    )(cc, sc, wbd)


def _fft_tables(seq):
    n1 = int(round(math.sqrt(seq)))
    n2 = seq // n1
    assert n1 * n2 == seq and n1 % FFT_ROWS == 0 and (n2 * GROUP_W) % FFT_COLS == 0
    scale = 1.0 / math.sqrt(seq * SUBHEAD_DIM)
    a = np.arange(n1)[:, None, None]
    k2 = np.arange(n2)[None, :, None]
    b = np.arange(n2)[None, None, :]
    ang = 2.0 * np.pi * (((a + n1 * b) * k2) % seq) / seq
    g_re = np.cos(ang) * scale
    g_im = -np.sin(ang) * scale
    k1 = np.arange(n1)
    ang2 = 2.0 * np.pi * (np.outer(k1, k1) % n1) / n1
    to = lambda t: jnp.asarray(t, F32).astype(BF16)
    return n1, n2, to(g_re), to(g_im), to(np.cos(ang2)), to(np.sin(ang2))


def _fft1_kernel(z_ref, gr_ref, gi_ref, vr_ref, vi_ref):
    for j in range(FFT_ROWS):
        xs = z_ref[0, :, j, :].astype(BF16)
        vr_ref[0, j] = _dot(gr_ref[j], xs).astype(BF16)
        vi_ref[0, j] = _dot(gi_ref[j], xs).astype(BF16)


def _fft2_kernel(vr_ref, vi_ref, c_ref, s_ref, m1_ref, m2_ref, o_ref):
    vr = vr_ref[0]
    vi = vi_ref[0]
    c = c_ref[...]
    s = s_ref[...]
    yr = _dot(c, vr) + _dot(s, vi)
    yi = _dot(c, vi) - _dot(s, vr)
    m1 = m1_ref[0]
    m2 = m2_ref[0]
    for g in range(FFT_COLS // GROUP_W):
        sl = slice(g * GROUP_W, (g + 1) * GROUP_W)
        o_ref[0, :, sl] = (_dot(yr[:, sl].astype(BF16), m1) + _dot(yi[:, sl].astype(BF16), m2)).astype(BF16)


def _fourier(z, batch, seq, tables, m1, m2, layer):
    n1, n2, g_re, g_im, c2, s2 = tables
    z4 = z.reshape(batch, n2, n1, IN_COLS)
    v_shape = jax.ShapeDtypeStruct((batch, n1, n2, GROUP_W), BF16)
    g_spec = pl.BlockSpec((FFT_ROWS, n2, n2), lambda b, i: (i, 0, 0))
    v_spec = pl.BlockSpec((1, FFT_ROWS, n2, GROUP_W), lambda b, i: (b, i, 0, 0))
    vr, vi = pl.pallas_call(
        _fft1_kernel,
        grid=(batch, n1 // FFT_ROWS),
        in_specs=[pl.BlockSpec((1, n2, FFT_ROWS, GROUP_W), lambda b, i: (b, 0, i, 0)), g_spec, g_spec],
        out_specs=[v_spec, v_spec],
        out_shape=[v_shape, v_shape],
        compiler_params=_params("arbitrary", "arbitrary"),
        name="fft_stage1",
    )(z4, g_re, g_im)
    width = n2 * GROUP_W
    vr = vr.reshape(batch, n1, width)
    vi = vi.reshape(batch, n1, width)
    blk = pl.BlockSpec((1, n1, FFT_COLS), lambda b, j: (b, 0, j))
    tab = pl.BlockSpec((n1, n1), lambda b, j: (0, 0))
    mat = pl.BlockSpec((1, GROUP_W, GROUP_W), lambda b, j: (layer, 0, 0))
    oa = pl.pallas_call(
        _fft2_kernel,
        grid=(batch, width // FFT_COLS),
        in_specs=[blk, blk, tab, tab, mat, mat],
        out_specs=blk,
        out_shape=jax.ShapeDtypeStruct((batch, n1, width), BF16),
        compiler_params=_params("arbitrary", "arbitrary"),
        name="fft_stage2",
    )(vr, vi, c2, s2, m1, m2)
    return oa.reshape(batch, seq, GROUP_W)


def _inproj_kernel(x_ref, mod_ref, w_ref, b_ref, z_ref):
    m = mod_ref[0]
    h = _ln(x_ref[0]) * (1.0 + m[1:2]) + m[0:1]
    z_ref[0] = _dot(h.astype(BF16), w_ref[0]) + b_ref[0]


def _inproj(x, mod, w_in, b_in, layer, tile):
    batch, seq, _ = x.shape
    return pl.pallas_call(
        _inproj_kernel,
        grid=(batch, seq // tile),
        in_specs=[pl.BlockSpec((1, tile, D_MODEL), lambda b, i: (b, i, 0)),
                  pl.BlockSpec((1, 6, D_MODEL), lambda b, i: (b, 0, 0)),
                  pl.BlockSpec((1, D_MODEL, IN_COLS), lambda b, i: (layer, 0, 0)),
                  pl.BlockSpec((1, 1, IN_COLS), lambda b, i: (layer, 0, 0))],
        out_specs=pl.BlockSpec((1, tile, IN_COLS), lambda b, i: (b, i, 0)),
        out_shape=jax.ShapeDtypeStruct((batch, seq, IN_COLS), F32),
        compiler_params=_params("arbitrary", "arbitrary"),
        name="in_proj",
    )(x, mod, w_in, b_in)


def _mix_kernel(seq, tile,
                x_ref, oa_ref,
                bvp_ref, bvm_ref, bvn_ref, bgp_ref, bgm_ref, bgn_ref,
                pcp_ref, pcm_ref, pcn_ref, du_ref, dv_ref, mod_ref,
                convw_ref, convb_ref, clng_ref, clnb_ref, cpw_ref,
                pmask_ref, poolw_ref, pscale_ref,
                slng_ref, slnb_ref, wst_ref, sbias_ref, avg_ref,
                wout_ref, bout_ref, ln1g_ref, ln1b_ref,
                rwh_ref, rwl_ref, rb_ref, tri_ref,
                x1_ref, h2_ref, idx_ref, wts_ref, lrank_ref, cnt_ref,
                gbuf, pbuf):
    i = pl.program_id(1)
    first = i == 0
    last = i == pl.num_programs(1) - 1
    m = mod_ref[0]

    def glu(v_ref, g_ref):
        return v_ref[0] * jax.nn.sigmoid(g_ref[0])

    gbuf[0:HALO] = jnp.where(first, 0.0, glu(bvp_ref, bgp_ref))
    gbuf[HALO:HALO + tile] = glu(bvm_ref, bgm_ref)
    gbuf[HALO + tile:] = jnp.where(last, 0.0, glu(bvn_ref, bgn_ref))
    acc = jnp.zeros((tile, GROUP_W), F32)
    off = HALO - CONV_WIDTH // 2
    for j in range(CONV_WIDTH):
        acc = acc + gbuf[off + j:off + j + tile, :] * convw_ref[0, j:j + 1, :]
    cv = _ln(acc + convb_ref[0]) * clng_ref[0] + clnb_ref[0]
    cv = cv * jax.nn.sigmoid(cv)
    ob = _dot(cv.astype(BF16), cpw_ref[0])

    pc = pcm_ref[0]
    pbuf[0:HALO] = jnp.where(first, 0.0, pcp_ref[0])
    pbuf[HALO:HALO + tile] = pc
    pbuf[HALO + tile:] = jnp.where(last, 0.0, pcn_ref[0])
    ssum = jnp.zeros((tile, GROUP_W), F32)
    off = HALO - POOL_TAPS // 2
    for j in range(POOL_TAPS):
        ssum = ssum + pbuf[off + j:off + j + tile, :] * pmask_ref[j:j + 1, :]
    lane = lax.broadcasted_iota(jnp.int32, (tile, GROUP_W), 1)
    half = jnp.left_shift(1, lane // SUBHEAD_DIM)
    pos = i * tile + lax.broadcasted_iota(jnp.int32, (tile, GROUP_W), 0)
    cnt = jnp.minimum(pos + half, seq) - jnp.maximum(pos - half, 0)
    pooled = ssum / cnt.astype(F32) - pc
    oc = _dot(pooled.astype(BF16), poolw_ref[0]) * pscale_ref[0]

    v = dv_ref[0]
    avg = avg_ref[...]
    v_hi, v_lo = _split(v)
    dv = v - (_dot(v_hi, avg) + _dot(v_lo, avg))
    q_hi, q_lo = _split(dv * dv)
    var = _dot(q_hi, avg) + _dot(q_lo, avg)
    vn = (dv * lax.rsqrt(var + LN_EPS) * slng_ref[0] + slnb_ref[0]).astype(BF16)
    wst = wst_ref[0]
    lane_c = lax.broadcasted_iota(jnp.int32, (CHUNK, GROUP_W), 1) // SUBHEAD_DIM
    mixed = []
    for c in range(tile // CHUNK):
        res = _dot(wst, vn[c * CHUNK:(c + 1) * CHUNK])
        sel = res[0:CHUNK]
        for h in range(1, N_SUBHEADS):
            sel = jnp.where(lane_c == h, res[h * CHUNK:(h + 1) * CHUNK], sel)
        mixed.append(sel + sbias_ref[0])
    od = du_ref[0] * jnp.concatenate(mixed, axis=0)

    g = GROUP_W
    mo = (_dot(oa_ref[0], wout_ref[0, 0:g]) + _dot(ob.astype(BF16), wout_ref[0, g:2 * g])
          + _dot(oc.astype(BF16), wout_ref[0, 2 * g:3 * g]) + _dot(od.astype(BF16), wout_ref[0, 3 * g:4 * g])
          + bout_ref[0])
    x1 = _ln(DN_ALPHA * x_ref[0] + m[2:3] * mo) * ln1g_ref[0] + ln1b_ref[0]
    x1_ref[0] = x1

    h2 = _ln(x1) * (1.0 + m[4:5]) + m[3:4]
    h2_ref[0] = h2
    logits = _dot3(h2, rwh_ref[0], rwl_ref[0]) + rb_ref[0]
    eio = lax.broadcasted_iota(jnp.int32, (tile, N_EXPERTS), 1)
    kio = lax.broadcasted_iota(jnp.int32, (tile, TOP_K), 1)
    work = logits
    picks, vals, hots = [], [], []
    for _ in range(TOP_K):
        mx = jnp.max(work, axis=1, keepdims=True)
        am = jnp.min(jnp.where(work == mx, eio, N_EXPERTS), axis=1, keepdims=True)
        hot = eio == am
        picks.append(am)
        vals.append(mx)
        hots.append(hot)
        work = jnp.where(hot, -jnp.inf, work)
    exps = [jnp.exp(vv - vals[0]) for vv in vals]
    denom = exps[0] + exps[1] + exps[2] + exps[3]
    onehot = jnp.zeros((tile, N_EXPERTS), F32)
    for hot in hots:
        onehot = onehot + hot.astype(F32)
    before = _dot(tri_ref[...], onehot.astype(BF16))
    idx = jnp.zeros((tile, TOP_K), jnp.int32)
    wts = jnp.zeros((tile, TOP_K), F32)
    lrank = jnp.zeros((tile, TOP_K), jnp.int32)
    for k in range(TOP_K):
        rk = jnp.sum(jnp.where(hots[k], before, 0.0), axis=1, keepdims=True).astype(jnp.int32)
        idx = jnp.where(kio == k, picks[k], idx)
        wts = jnp.where(kio == k, exps[k] / denom, wts)
        lrank = jnp.where(kio == k, rk, lrank)
    idx_ref[0] = idx
    wts_ref[0] = wts
    lrank_ref[0] = lrank
    cnt_ref[0, 0] = jnp.sum(onehot, axis=0, keepdims=True).astype(jnp.int32)


def _mix(x, z, oa, mod, p, layer, tile):
    batch, seq, _ = x.shape
    nt = seq // tile
    hb = tile // HALO
    nhb = seq // HALO

    def col(c):
        return pl.BlockSpec((1, tile, GROUP_W), lambda b, i: (b, i, c))

    def prev(c):
        return pl.BlockSpec((1, HALO, GROUP_W), lambda b, i: (b, jnp.maximum(i * hb - 1, 0), c))

    def nxt(c):
        return pl.BlockSpec((1, HALO, GROUP_W), lambda b, i: (b, jnp.minimum((i + 1) * hb, nhb - 1), c))

    def lay(*shape):
        return pl.BlockSpec((1,) + shape, lambda b, i: (layer,) + (0,) * len(shape))

    def const(*shape):
        return pl.BlockSpec(shape, lambda b, i: (0,) * len(shape))

    row = pl.BlockSpec((1, tile, D_MODEL), lambda b, i: (b, i, 0))
    sel = pl.BlockSpec((1, tile, TOP_K), lambda b, i: (b, i, 0))
    in_specs = [
        row, col(0),
        prev(1), col(1), nxt(1), prev(2), col(2), nxt(2),
        prev(3), col(3), nxt(3), col(4), col(5),
        pl.BlockSpec((1, 6, D_MODEL), lambda b, i: (b, 0, 0)),
        lay(CONV_WIDTH, GROUP_W), lay(1, GROUP_W), lay(1, GROUP_W), lay(1, GROUP_W), lay(GROUP_W, GROUP_W),
        const(POOL_TAPS, GROUP_W), lay(GROUP_W, GROUP_W), lay(1, GROUP_W),
        lay(1, GROUP_W), lay(1, GROUP_W), lay(N_SUBHEADS * CHUNK, CHUNK), lay(CHUNK, GROUP_W),
        const(GROUP_W, GROUP_W),
        lay(D_MODEL, D_MODEL), lay(1, D_MODEL), lay(1, D_MODEL), lay(1, D_MODEL),
        lay(D_MODEL, N_EXPERTS), lay(D_MODEL, N_EXPERTS), lay(1, N_EXPERTS), const(tile, tile),
    ]
    out_specs = [row, row, sel, sel, sel,
                 pl.BlockSpec((1, 1, 1, N_EXPERTS), lambda b, i: (b, i, 0, 0))]
    out_shape = [jax.ShapeDtypeStruct((batch, seq, D_MODEL), F32),
                 jax.ShapeDtypeStruct((batch, seq, D_MODEL), F32),
                 jax.ShapeDtypeStruct((batch, seq, TOP_K), jnp.int32),
                 jax.ShapeDtypeStruct((batch, seq, TOP_K), F32),
                 jax.ShapeDtypeStruct((batch, seq, TOP_K), jnp.int32),
                 jax.ShapeDtypeStruct((batch, nt, 1, N_EXPERTS), jnp.int32)]
    tri = jnp.asarray(np.tril(np.ones((tile, tile), np.float32), -1), BF16)
    return pl.pallas_call(
        functools.partial(_mix_kernel, seq, tile),
        grid=(batch, nt),
        in_specs=in_specs,
        out_specs=out_specs,
        out_shape=out_shape,
        scratch_shapes=[pltpu.VMEM((tile + 2 * HALO, GROUP_W), F32),
                        pltpu.VMEM((tile + 2 * HALO, GROUP_W), F32)],
        compiler_params=_params("arbitrary", "arbitrary"),
        name="mix",
    )(x, oa, z, z, z, z, z, z, z, z, z, z, z, mod,
      p["conv_w"], p["conv_b"], p["conv_ln_g"], p["conv_ln_b"], p["conv_pw"],
      p["pool_mask"], p["pool_w"], p["pool_scale"],
      p["sgu_ln_g"], p["sgu_ln_b"], p["sgu_w"], p["sgu_b"], p["avg"],
      p["w_out"], p["b_out"], p["ln1_g"], p["ln1_b"],
      p["router_hi"], p["router_lo"], p["router_b"], tri)


def _route(idx, lrank, cnt, tile, rows):
    n = idx.shape[0] * idx.shape[1]
    idx = idx.reshape(n, TOP_K)
    lrank = lrank.reshape(n, TOP_K)
    cnt = cnt.reshape(-1, N_EXPERTS)
    total = jnp.sum(cnt, axis=0)
    padded = ((total + rows - 1) // rows) * rows
    cum_pad = jnp.cumsum(padded)
    base = (cum_pad - padded)[None, :] + jnp.cumsum(cnt, axis=0) - cnt
    base_tok = jnp.repeat(base, tile, axis=0)
    dest = jnp.take_along_axis(base_tok, idx, axis=1) + lrank
    n_blocks = n * TOP_K // rows + N_EXPERTS
    block_start = jnp.arange(n_blocks, dtype=cum_pad.dtype) * rows
    block_e = jnp.minimum(jnp.sum(cum_pad[None, :] <= block_start[:, None], axis=1),
                          N_EXPERTS - 1).astype(jnp.int32)
    return dest.reshape(-1).astype(jnp.int32), block_e, n_blocks


def _row_copy(src, src_row, dst, dst_row, sem):
    return pltpu.make_async_copy(src.at[pl.ds(src_row, 1)], dst.at[pl.ds(dst_row, 1)], sem)


def _dispatch_kernel(tile, dest_ref, h2_ref, xs_in, xs_hbm, sem):
    del xs_in

    def body(n8, carry):
        for j in range(DMA_UNROLL):
            n = n8 * DMA_UNROLL + j
            for k in range(TOP_K):
                _row_copy(h2_ref, n, xs_hbm, dest_ref[n * TOP_K + k], sem).start()
        return carry

    lax.fori_loop(0, tile // DMA_UNROLL, body, 0)
    for _ in range(TOP_K):
        pltpu.make_async_copy(h2_ref, xs_hbm.at[pl.ds(0, tile)], sem).wait()


def _dispatch(dest, h2, total_rows, tile):
    n = h2.shape[0]
    xs0 = jnp.zeros((total_rows, D_MODEL), F32)
    return pl.pallas_call(
        functools.partial(_dispatch_kernel, tile),
        grid=(n // tile,),
        in_specs=[pl.BlockSpec((tile * TOP_K,), lambda i: (i,), memory_space=pltpu.SMEM),
                  pl.BlockSpec((tile, D_MODEL), lambda i: (i, 0)),
                  pl.BlockSpec(memory_space=pl.ANY)],
        out_specs=pl.BlockSpec(memory_space=pl.ANY),
        out_shape=jax.ShapeDtypeStruct((total_rows, D_MODEL), F32),
        scratch_shapes=[pltpu.SemaphoreType.DMA],
        input_output_aliases={2: 0},
        compiler_params=_params("arbitrary"),
        name="dispatch",
    )(dest, h2, xs0)


def _gu_prep_kernel(w_ref, perm_ref, o_ref):
    o_ref[0, 0] = _dot(w_ref[0, 0].astype(BF16), perm_ref[...]).astype(BF16)


def _gu_prep(w_gu):
    nl, ne, d, c = w_gu.shape
    perm = np.zeros((c, c), np.float32)
    perm[_GU_ORDER, np.arange(c)] = 1.0
    blk = pl.BlockSpec((1, 1, d, c), lambda l, e: (l, e, 0, 0))
    return pl.pallas_call(
        _gu_prep_kernel,
        grid=(nl, ne),
        in_specs=[blk, pl.BlockSpec((c, c), lambda l, e: (0, 0))],
        out_specs=blk,
        out_shape=jax.ShapeDtypeStruct(w_gu.shape, BF16),
        compiler_params=_params("arbitrary", "arbitrary"),
        name="gu_prep",
    )(w_gu, jnp.asarray(perm, BF16))


def _expert_kernel(be_ref, xs_ref, wgu_ref, bgu_ref, wd_ref, bd_ref, y_ref):
    del be_ref
    hu = _dot(xs_ref[...].astype(BF16), wgu_ref[0, 0]) + bgu_ref[0, 0]
    glu = jnp.minimum(hu[:, :D_EXPERT], SWIGLU_LIMIT)
    lin = jnp.clip(hu[:, D_EXPERT:], -SWIGLU_LIMIT, SWIGLU_LIMIT)
    act = glu * jax.nn.sigmoid(SWIGLU_ALPHA * glu) * (lin + 1.0)
    y_ref[...] = _dot(act.astype(BF16), wd_ref[0, 0]) + bd_ref[0, 0]


def _experts(block_e, xs, p, layer, rows, n_blocks):
    def wspec(*shape):
        return pl.BlockSpec((1, 1) + shape, lambda g, be: (layer, be[g]) + (0,) * len(shape))

    grid_spec = pltpu.PrefetchScalarGridSpec(
        num_scalar_prefetch=1,
        grid=(n_blocks,),
        in_specs=[pl.BlockSpec((rows, D_MODEL), lambda g, be: (g, 0)),
                  wspec(D_MODEL, 2 * D_EXPERT), wspec(1, 2 * D_EXPERT),
                  wspec(D_EXPERT, D_MODEL), wspec(1, D_MODEL)],
        out_specs=pl.BlockSpec((rows, D_MODEL), lambda g, be: (g, 0)),
    )
    return pl.pallas_call(
        _expert_kernel,
        grid_spec=grid_spec,
        out_shape=jax.ShapeDtypeStruct(xs.shape, F32),
        compiler_params=_params("arbitrary"),
        name="experts",
    )(block_e, xs, p["w_gu"], p["b_gu"], p["w_dn"], p["b_dn"])


def _combine_kernel(tile, dest_ref, y_hbm, x1_ref, wts_ref, mod_ref, g_ref, b_ref, o_ref, buf, sem):
    def body(n8, carry):
        for j in range(DMA_UNROLL):
            n = n8 * DMA_UNROLL + j
            for k in range(TOP_K):
                _row_copy(y_hbm, dest_ref[n * TOP_K + k], buf, k * tile + n, sem).start()
        return carry

    lax.fori_loop(0, tile // DMA_UNROLL, body, 0)
    pltpu.make_async_copy(y_hbm.at[pl.ds(0, tile * TOP_K)], buf, sem).wait()
    wts = wts_ref[0]
    f = wts[:, 0:1] * buf[0:tile]
    for k in range(1, TOP_K):
        f = f + wts[:, k:k + 1] * buf[k * tile:(k + 1) * tile]
    m = mod_ref[0]
    o_ref[0] = _ln(DN_ALPHA * x1_ref[0] + m[5:6] * f) * g_ref[0] + b_ref[0]


def _combine(dest, y, x1, wts, mod, p, layer, tile):
    batch, seq, _ = x1.shape
    nt = seq // tile
    row = pl.BlockSpec((1, tile, D_MODEL), lambda b, i: (b, i, 0))
    vec = pl.BlockSpec((1, 1, D_MODEL), lambda b, i: (layer, 0, 0))
    return pl.pallas_call(
        functools.partial(_combine_kernel, tile),
        grid=(batch, nt),
        in_specs=[pl.BlockSpec((tile * TOP_K,), lambda b, i: (b * nt + i,), memory_space=pltpu.SMEM),
                  pl.BlockSpec(memory_space=pl.ANY),
                  row,
                  pl.BlockSpec((1, tile, TOP_K), lambda b, i: (b, i, 0)),
                  pl.BlockSpec((1, 6, D_MODEL), lambda b, i: (b, 0, 0)),
                  vec, vec],
        out_specs=row,
        out_shape=jax.ShapeDtypeStruct(x1.shape, F32),
        scratch_shapes=[pltpu.VMEM((tile * TOP_K, D_MODEL), F32), pltpu.SemaphoreType.DMA],
        compiler_params=_params("arbitrary", "arbitrary"),
        name="combine",
    )(dest, y, x1, wts, mod, p["ln2_g"], p["ln2_b"])


def _prepare(w_in, b_in, fnet_w, conv_w, conv_b, conv_ln_g, conv_ln_b, conv_pw, pool_w, pool_scale,
             sgu_ln_g, sgu_ln_b, sgu_w, sgu_b, w_out, b_out, ln1_g, ln1_b, router_w, router_b,
             w_gu, b_gu, w_dn, b_dn, ln2_g, ln2_b):
    nl = w_in.shape[0]
    vec = lambda t: t.reshape(nl, 1, -1)
    half = np.repeat(np.array(POOL_WINDOWS) // 2, SUBHEAD_DIM)[None, :]
    tap = np.arange(POOL_TAPS)[:, None] - POOL_TAPS // 2
    pool_mask = ((tap >= -half) & (tap < half)).astype(np.float32)
    group = np.arange(GROUP_W) // SUBHEAD_DIM
    avg = (group[:, None] == group[None, :]).astype(np.float32) / SUBHEAD_DIM
    r_hi = router_w.astype(BF16)
    m1, m2 = _fnet_prep(fnet_w)
    return {
        "w_in": w_in.astype(BF16), "b_in": vec(b_in), "fnet_m1": m1, "fnet_m2": m2,
        "conv_w": conv_w, "conv_b": vec(conv_b), "conv_ln_g": vec(conv_ln_g), "conv_ln_b": vec(conv_ln_b),
        "conv_pw": conv_pw.astype(BF16),
        "pool_mask": jnp.asarray(pool_mask), "pool_w": _block_diag(pool_w).astype(BF16),
        "pool_scale": vec(pool_scale),
        "sgu_ln_g": vec(sgu_ln_g), "sgu_ln_b": vec(sgu_ln_b),
        "sgu_w": sgu_w.reshape(nl, N_SUBHEADS * CHUNK, CHUNK).astype(BF16),
        "sgu_b": jnp.repeat(jnp.swapaxes(sgu_b, 1, 2), SUBHEAD_DIM, axis=2),
        "avg": jnp.asarray(avg, BF16),
        "w_out": w_out.astype(BF16), "b_out": vec(b_out), "ln1_g": vec(ln1_g), "ln1_b": vec(ln1_b),
        "router_hi": r_hi, "router_lo": (router_w - r_hi.astype(F32)).astype(BF16), "router_b": vec(router_b),
        "w_gu": _gu_prep(w_gu), "b_gu": b_gu[..., None, _GU_ORDER],
        "w_dn": w_dn.astype(BF16), "b_dn": b_dn[..., None, :],
        "ln2_g": vec(ln2_g), "ln2_b": vec(ln2_b),
    }


def _trunk(x, mod, p):
    batch, seq, _ = x.shape
    t_mix, t_cmb, rows = _tiles(seq)
    tables = _fft_tables(seq)
    for layer in range(DEPTH):
        z = _inproj(x, mod[layer], p["w_in"], p["b_in"], layer, t_mix)
        oa = _fourier(z, batch, seq, tables, p["fnet_m1"], p["fnet_m2"], layer)
        x1, h2, idx, wts, lrank, cnt = _mix(x, z, oa, mod[layer], p, layer, t_mix)
        dest, block_e, n_blocks = _route(idx, lrank, cnt, t_mix, rows)
        xs = _dispatch(dest, h2.reshape(batch * seq, D_MODEL), n_blocks * rows, t_mix)
        y = _experts(block_e, xs, p, layer, rows, n_blocks)
        x = _combine(dest, y, x1, wts, mod[layer], p, layer, t_cmb)
    return x


def kernel(x_prompt, x_sample, c_prompt, c_sample, w_mod, b_mod, w_in, b_in, fnet_w, conv_w, conv_b,
           conv_ln_g, conv_ln_b, conv_pw, pool_w, pool_scale, sgu_ln_g, sgu_ln_b, sgu_w, sgu_b,
           w_out, b_out, ln1_g, ln1_b, router_w, router_b, w_gu, b_gu, w_dn, b_dn, ln2_g, ln2_b):
    p = _prepare(w_in, b_in, fnet_w, conv_w, conv_b, conv_ln_g, conv_ln_b, conv_pw, pool_w, pool_scale,
                 sgu_ln_g, sgu_ln_b, sgu_w, sgu_b, w_out, b_out, ln1_g, ln1_b, router_w, router_b,
                 w_gu, b_gu, w_dn, b_dn, ln2_g, ln2_b)
    nb_p, nb_s = c_prompt.shape[0], c_sample.shape[0]
    pad = (-(nb_p + nb_s)) % 8
    c_all = jnp.concatenate([c_prompt, c_sample, jnp.zeros((pad, D_MODEL), F32)], axis=0)
    mod = _modulation(c_all, w_mod, b_mod)
    nl = mod.shape[0]
    mod_p = mod[:, :nb_p].reshape(nl, nb_p, 6, D_MODEL)
    mod_s = mod[:, nb_p:nb_p + nb_s].reshape(nl, nb_s, 6, D_MODEL)
    return _trunk(x_prompt, mod_p, p), _trunk(x_sample, mod_s, p)
```

```python
import functools
import math

import numpy as np
import jax
import jax.numpy as jnp
from jax import lax
from jax.experimental import pallas as pl
from jax.experimental.pallas import tpu as pltpu

D_MODEL = 1024
DEPTH = 4
GROUP_W = 256
N_SUBHEADS = 4
SUBHEAD_DIM = 64
IN_COLS = 6 * GROUP_W
CONV_WIDTH = 31
POOL_WINDOWS = (2, 4, 8, 16)
CHUNK = 128
N_EXPERTS = 32
TOP_K = 4
D_EXPERT = 512
SWIGLU_LIMIT = 7.0
SWIGLU_ALPHA = 1.702
DN_ALPHA = (2.0 * DEPTH) ** 0.25
LN_EPS = 1e-5

HALO = 16
POOL_TAPS = 16
FFT_ROWS = 8
FFT_COLS = 2048
SUBLANES = 8
VMEM_LIMIT = 48 * 1024 * 1024

F32 = jnp.float32
BF16 = jnp.bfloat16

_GU_ORDER = np.concatenate([np.arange(0, 2 * D_EXPERT, 2), np.arange(1, 2 * D_EXPERT, 2)])


def _tiles(seq):
    t_mix = min(512, seq)
    t_cmb = min(256, seq)
    r_exp = 512 if seq >= 4096 else 256
    return t_mix, t_cmb, r_exp


def _dot(a, b):
    return jnp.dot(a, b, preferred_element_type=F32)


def _split(x):
    hi = x.astype(BF16)
    lo = (x - hi.astype(F32)).astype(BF16)
    return hi, lo


def _dot3(a, b_hi, b_lo):
    a_hi, a_lo = _split(a)
    return _dot(a_hi, b_hi) + _dot(a_lo, b_hi) + _dot(a_hi, b_lo)


def _ln(x):
    mu = jnp.mean(x, axis=-1, keepdims=True)
    d = x - mu
    var = jnp.mean(d * d, axis=-1, keepdims=True)
    return d * lax.rsqrt(var + LN_EPS)


def _params(*sem):
    return pltpu.CompilerParams(dimension_semantics=sem, vmem_limit_bytes=VMEM_LIMIT)


def _mod_kernel(c_ref, w_ref, b_ref, o_ref):
    c = c_ref[...]
    s = c * jax.nn.sigmoid(c)
    w_hi, w_lo = _split(w_ref[0])
    o_ref[0] = _dot3(s, w_hi, w_lo) + b_ref[0]


def _modulation(c_all, w_mod, b_mod):
    bp = c_all.shape[0]
    nl = w_mod.shape[0]
    ncol = w_mod.shape[2] // D_MODEL
    return pl.pallas_call(
        _mod_kernel,
        grid=(nl, ncol),
        in_specs=[pl.BlockSpec((bp, D_MODEL), lambda l, j: (0, 0)),
                  pl.BlockSpec((1, D_MODEL, D_MODEL), lambda l, j: (l, 0, j)),
                  pl.BlockSpec((1, 1, D_MODEL), lambda l, j: (l, 0, j))],
        out_specs=pl.BlockSpec((1, bp, D_MODEL), lambda l, j: (l, 0, j)),
        out_shape=jax.ShapeDtypeStruct((nl, bp, w_mod.shape[2]), F32),
        compiler_params=_params("arbitrary", "arbitrary"),
        name="modulation",
    )(c_all, w_mod, b_mod.reshape(nl, 1, -1))


def _fnet_prep_kernel(cc_ref, sc_ref, w_ref, m1_ref, m2_ref):
    w_hi, w_lo = _split(w_ref[0])
    m1_ref[0] = _dot3(cc_ref[...], w_hi, w_lo).astype(BF16)
    m2_ref[0] = _dot3(sc_ref[...], w_hi, w_lo).astype(BF16)


def _block_diag(w):
    nl, h, c, _ = w.shape
    eye = jnp.eye(h, dtype=w.dtype)
    return jnp.einsum("lhcd,hg->lhcgd", w, eye).reshape(nl, h * c, h * c)


def _fnet_prep(fnet_w):
    nl = fnet_w.shape[0]
    k = np.arange(SUBHEAD_DIM)
    ang = 2.0 * np.pi * np.outer(k, k) / SUBHEAD_DIM
    eye = np.eye(N_SUBHEADS)
    cc = jnp.asarray(np.kron(eye, np.cos(ang)), F32)
    sc = jnp.asarray(np.kron(eye, np.sin(ang)), F32)
    wbd = _block_diag(fnet_w)
    mat = pl.BlockSpec((GROUP_W, GROUP_W), lambda l: (0, 0))
    per_layer = pl.BlockSpec((1, GROUP_W, GROUP_W), lambda l: (l, 0, 0))
    return pl.pallas_call(
        _fnet_prep_kernel,
        grid=(nl,),
        in_specs=[mat, mat, per_layer],
        out_specs=[per_layer, per_layer],
        out_shape=[jax.ShapeDtypeStruct((nl, GROUP_W, GROUP_W), BF16)] * 2,
        compiler_params=_params("arbitrary"),
        name="fnet_prep",
    )(cc, sc, wbd)


def _fft_tables(seq):
    n1 = int(round(math.sqrt(seq)))
    n2 = seq // n1
    assert n1 * n2 == seq and n1 % FFT_ROWS == 0 and (n2 * GROUP_W) % FFT_COLS == 0
    scale = 1.0 / math.sqrt(seq * SUBHEAD_DIM)
    a = np.arange(n1)[:, None, None]
    k2 = np.arange(n2)[None, :, None]
    b = np.arange(n2)[None, None, :]
    ang = 2.0 * np.pi * (((a + n1 * b) * k2) % seq) / seq
    g_re = np.cos(ang) * scale
    g_im = -np.sin(ang) * scale
    k1 = np.arange(n1)
    ang2 = 2.0 * np.pi * (np.outer(k1, k1) % n1) / n1
    to = lambda t: jnp.asarray(t, F32).astype(BF16)
    return n1, n2, to(g_re), to(g_im), to(np.cos(ang2)), to(np.sin(ang2))


def _fft1_kernel(z_ref, gr_ref, gi_ref, vr_ref, vi_ref):
    for j in range(FFT_ROWS):
        xs = z_ref[0, :, j, :].astype(BF16)
        vr_ref[0, j] = _dot(gr_ref[j], xs).astype(BF16)
        vi_ref[0, j] = _dot(gi_ref[j], xs).astype(BF16)


def _fft2_kernel(vr_ref, vi_ref, c_ref, s_ref, m1_ref, m2_ref, o_ref):
    vr = vr_ref[0]
    vi = vi_ref[0]
    c = c_ref[...]
    s = s_ref[...]
    yr = _dot(c, vr) + _dot(s, vi)
    yi = _dot(c, vi) - _dot(s, vr)
    m1 = m1_ref[0]
    m2 = m2_ref[0]
    for g in range(FFT_COLS // GROUP_W):
        sl = slice(g * GROUP_W, (g + 1) * GROUP_W)
        o_ref[0, :, sl] = (_dot(yr[:, sl].astype(BF16), m1) + _dot(yi[:, sl].astype(BF16), m2)).astype(BF16)


def _fourier(z, batch, seq, tables, m1, m2, layer):
    n1, n2, g_re, g_im, c2, s2 = tables
    z4 = z.reshape(batch, n2, n1, IN_COLS)
    v_shape = jax.ShapeDtypeStruct((batch, n1, n2, GROUP_W), BF16)
    g_spec = pl.BlockSpec((FFT_ROWS, n2, n2), lambda b, i: (i, 0, 0))
    v_spec = pl.BlockSpec((1, FFT_ROWS, n2, GROUP_W), lambda b, i: (b, i, 0, 0))
    vr, vi = pl.pallas_call(
        _fft1_kernel,
        grid=(batch, n1 // FFT_ROWS),
        in_specs=[pl.BlockSpec((1, n2, FFT_ROWS, GROUP_W), lambda b, i: (b, 0, i, 0)), g_spec, g_spec],
        out_specs=[v_spec, v_spec],
        out_shape=[v_shape, v_shape],
        compiler_params=_params("arbitrary", "arbitrary"),
        name="fft_stage1",
    )(z4, g_re, g_im)
    width = n2 * GROUP_W
    vr = vr.reshape(batch, n1, width)
    vi = vi.reshape(batch, n1, width)
    blk = pl.BlockSpec((1, n1, FFT_COLS), lambda b, j: (b, 0, j))
    tab = pl.BlockSpec((n1, n1), lambda b, j: (0, 0))
    mat = pl.BlockSpec((1, GROUP_W, GROUP_W), lambda b, j: (layer, 0, 0))
    oa = pl.pallas_call(
        _fft2_kernel,
        grid=(batch, width // FFT_COLS),
        in_specs=[blk, blk, tab, tab, mat, mat],
        out_specs=blk,
        out_shape=jax.ShapeDtypeStruct((batch, n1, width), BF16),
        compiler_params=_params("arbitrary", "arbitrary"),
        name="fft_stage2",
    )(vr, vi, c2, s2, m1, m2)
    return oa.reshape(batch, seq, GROUP_W)


def _inproj_kernel(x_ref, mod_ref, w_ref, b_ref, z_ref):
    m = mod_ref[0]
    h = _ln(x_ref[0]) * (1.0 + m[1:2]) + m[0:1]
    z_ref[0] = _dot(h.astype(BF16), w_ref[0]) + b_ref[0]


def _inproj(x, mod, w_in, b_in, layer, tile):
    batch, seq, _ = x.shape
    return pl.pallas_call(
        _inproj_kernel,
        grid=(batch, seq // tile),
        in_specs=[pl.BlockSpec((1, tile, D_MODEL), lambda b, i: (b, i, 0)),
                  pl.BlockSpec((1, 6, D_MODEL), lambda b, i: (b, 0, 0)),
                  pl.BlockSpec((1, D_MODEL, IN_COLS), lambda b, i: (layer, 0, 0)),
                  pl.BlockSpec((1, 1, IN_COLS), lambda b, i: (layer, 0, 0))],
        out_specs=pl.BlockSpec((1, tile, IN_COLS), lambda b, i: (b, i, 0)),
        out_shape=jax.ShapeDtypeStruct((batch, seq, IN_COLS), F32),
        compiler_params=_params("arbitrary", "arbitrary"),
        name="in_proj",
    )(x, mod, w_in, b_in)


def _mix_kernel(seq, tile,
                x_ref, oa_ref,
                bvp_ref, bvm_ref, bvn_ref, bgp_ref, bgm_ref, bgn_ref,
                pcp_ref, pcm_ref, pcn_ref, du_ref, dv_ref, mod_ref,
                convw_ref, convb_ref, clng_ref, clnb_ref, cpw_ref,
                pmask_ref, poolw_ref, pscale_ref,
                slng_ref, slnb_ref, wst_ref, sbias_ref, avg_ref,
                wout_ref, bout_ref, ln1g_ref, ln1b_ref,
                rwh_ref, rwl_ref, rb_ref, tri_ref,
                x1_ref, h2_ref, idx_ref, wts_ref, lrank_ref, cnt_ref,
                gbuf, pbuf):
    i = pl.program_id(1)
    first = i == 0
    last = i == pl.num_programs(1) - 1
    m = mod_ref[0]

    def glu(v_ref, g_ref):
        return v_ref[0] * jax.nn.sigmoid(g_ref[0])

    gbuf[0:HALO] = jnp.where(first, 0.0, glu(bvp_ref, bgp_ref))
    gbuf[HALO:HALO + tile] = glu(bvm_ref, bgm_ref)
    gbuf[HALO + tile:] = jnp.where(last, 0.0, glu(bvn_ref, bgn_ref))
    acc = jnp.zeros((tile, GROUP_W), F32)
    off = HALO - CONV_WIDTH // 2
    for j in range(CONV_WIDTH):
        acc = acc + gbuf[off + j:off + j + tile, :] * convw_ref[0, j:j + 1, :]
    cv = _ln(acc + convb_ref[0]) * clng_ref[0] + clnb_ref[0]
    cv = cv * jax.nn.sigmoid(cv)
    ob = _dot(cv.astype(BF16), cpw_ref[0])

    pc = pcm_ref[0]
    pbuf[0:HALO] = jnp.where(first, 0.0, pcp_ref[0])
    pbuf[HALO:HALO + tile] = pc
    pbuf[HALO + tile:] = jnp.where(last, 0.0, pcn_ref[0])
    ssum = jnp.zeros((tile, GROUP_W), F32)
    off = HALO - POOL_TAPS // 2
    for j in range(POOL_TAPS):
        ssum = ssum + pbuf[off + j:off + j + tile, :] * pmask_ref[j:j + 1, :]
    lane = lax.broadcasted_iota(jnp.int32, (tile, GROUP_W), 1)
    half = jnp.left_shift(1, lane // SUBHEAD_DIM)
    pos = i * tile + lax.broadcasted_iota(jnp.int32, (tile, GROUP_W), 0)
    cnt = jnp.minimum(pos + half, seq) - jnp.maximum(pos - half, 0)
    pooled = ssum / cnt.astype(F32) - pc
    oc = _dot(pooled.astype(BF16), poolw_ref[0]) * pscale_ref[0]

    v = dv_ref[0]
    avg = avg_ref[...]
    v_hi, v_lo = _split(v)
    dv = v - (_dot(v_hi, avg) + _dot(v_lo, avg))
    q_hi, q_lo = _split(dv * dv)
    var = _dot(q_hi, avg) + _dot(q_lo, avg)
    vn = (dv * lax.rsqrt(var + LN_EPS) * slng_ref[0] + slnb_ref[0]).astype(BF16)
    wst = wst_ref[0]
    lane_c = lax.broadcasted_iota(jnp.int32, (CHUNK, GROUP_W), 1) // SUBHEAD_DIM
    mixed = []
    for c in range(tile // CHUNK):
        res = _dot(wst, vn[c * CHUNK:(c + 1) * CHUNK])
        sel = res[0:CHUNK]
        for h in range(1, N_SUBHEADS):
            sel = jnp.where(lane_c == h, res[h * CHUNK:(h + 1) * CHUNK], sel)
        mixed.append(sel + sbias_ref[0])
    od = du_ref[0] * jnp.concatenate(mixed, axis=0)

    g = GROUP_W
    mo = (_dot(oa_ref[0], wout_ref[0, 0:g]) + _dot(ob.astype(BF16), wout_ref[0, g:2 * g])
          + _dot(oc.astype(BF16), wout_ref[0, 2 * g:3 * g]) + _dot(od.astype(BF16), wout_ref[0, 3 * g:4 * g])
          + bout_ref[0])
    x1 = _ln(DN_ALPHA * x_ref[0] + m[2:3] * mo) * ln1g_ref[0] + ln1b_ref[0]
    x1_ref[0] = x1

    h2 = _ln(x1) * (1.0 + m[4:5]) + m[3:4]
    h2_ref[0] = h2
    logits = _dot3(h2, rwh_ref[0], rwl_ref[0]) + rb_ref[0]
    eio = lax.broadcasted_iota(jnp.int32, (tile, N_EXPERTS), 1)
    kio = lax.broadcasted_iota(jnp.int32, (tile, TOP_K), 1)
    work = logits
    picks, vals, hots = [], [], []
    for _ in range(TOP_K):
        mx = jnp.max(work, axis=1, keepdims=True)
        am = jnp.min(jnp.where(work == mx, eio, N_EXPERTS), axis=1, keepdims=True)
        hot = eio == am
        picks.append(am)
        vals.append(mx)
        hots.append(hot)
        work = jnp.where(hot, -jnp.inf, work)
    exps = [jnp.exp(vv - vals[0]) for vv in vals]
    denom = exps[0] + exps[1] + exps[2] + exps[3]
    onehot = jnp.zeros((tile, N_EXPERTS), F32)
    for hot in hots:
        onehot = onehot + hot.astype(F32)
    before = _dot(tri_ref[...], onehot.astype(BF16))
    idx = jnp.zeros((tile, TOP_K), jnp.int32)
    wts = jnp.zeros((tile, TOP_K), F32)
    lrank = jnp.zeros((tile, TOP_K), jnp.int32)
    for k in range(TOP_K):
        rk = jnp.sum(jnp.where(hots[k], before, 0.0), axis=1, keepdims=True).astype(jnp.int32)
        idx = jnp.where(kio == k, picks[k], idx)
        wts = jnp.where(kio == k, exps[k] / denom, wts)
        lrank = jnp.where(kio == k, rk, lrank)
    idx_ref[0] = idx
    wts_ref[0] = wts
    lrank_ref[0] = lrank
    cnt_ref[0, 0] = jnp.sum(onehot, axis=0, keepdims=True).astype(jnp.int32)


def _mix(x, z, oa, mod, p, layer, tile):
    batch, seq, _ = x.shape
    nt = seq // tile
    hb = tile // HALO
    nhb = seq // HALO

    def col(c):
        return pl.BlockSpec((1, tile, GROUP_W), lambda b, i: (b, i, c))

    def prev(c):
        return pl.BlockSpec((1, HALO, GROUP_W), lambda b, i: (b, jnp.maximum(i * hb - 1, 0), c))

    def nxt(c):
        return pl.BlockSpec((1, HALO, GROUP_W), lambda b, i: (b, jnp.minimum((i + 1) * hb, nhb - 1), c))

    def lay(*shape):
        return pl.BlockSpec((1,) + shape, lambda b, i: (layer,) + (0,) * len(shape))

    def const(*shape):
        return pl.BlockSpec(shape, lambda b, i: (0,) * len(shape))

    row = pl.BlockSpec((1, tile, D_MODEL), lambda b, i: (b, i, 0))
    sel = pl.BlockSpec((1, tile, TOP_K), lambda b, i: (b, i, 0))
    in_specs = [
        row, col(0),
        prev(1), col(1), nxt(1), prev(2), col(2), nxt(2),
        prev(3), col(3), nxt(3), col(4), col(5),
        pl.BlockSpec((1, 6, D_MODEL), lambda b, i: (b, 0, 0)),
        lay(CONV_WIDTH, GROUP_W), lay(1, GROUP_W), lay(1, GROUP_W), lay(1, GROUP_W), lay(GROUP_W, GROUP_W),
        const(POOL_TAPS, GROUP_W), lay(GROUP_W, GROUP_W), lay(1, GROUP_W),
        lay(1, GROUP_W), lay(1, GROUP_W), lay(N_SUBHEADS * CHUNK, CHUNK), lay(CHUNK, GROUP_W),
        const(GROUP_W, GROUP_W),
        lay(D_MODEL, D_MODEL), lay(1, D_MODEL), lay(1, D_MODEL), lay(1, D_MODEL),
        lay(D_MODEL, N_EXPERTS), lay(D_MODEL, N_EXPERTS), lay(1, N_EXPERTS), const(tile, tile),
    ]
    out_specs = [row, row, sel, sel, sel,
                 pl.BlockSpec((1, 1, 1, N_EXPERTS), lambda b, i: (b, i, 0, 0))]
    out_shape = [jax.ShapeDtypeStruct((batch, seq, D_MODEL), F32),
                 jax.ShapeDtypeStruct((batch, seq, D_MODEL), F32),
                 jax.ShapeDtypeStruct((batch, seq, TOP_K), jnp.int32),
                 jax.ShapeDtypeStruct((batch, seq, TOP_K), F32),
                 jax.ShapeDtypeStruct((batch, seq, TOP_K), jnp.int32),
                 jax.ShapeDtypeStruct((batch, nt, 1, N_EXPERTS), jnp.int32)]
    tri = jnp.asarray(np.tril(np.ones((tile, tile), np.float32), -1), BF16)
    return pl.pallas_call(
        functools.partial(_mix_kernel, seq, tile),
        grid=(batch, nt),
        in_specs=in_specs,
        out_specs=out_specs,
        out_shape=out_shape,
        scratch_shapes=[pltpu.VMEM((tile + 2 * HALO, GROUP_W), F32),
                        pltpu.VMEM((tile + 2 * HALO, GROUP_W), F32)],
        compiler_params=_params("arbitrary", "arbitrary"),
        name="mix",
    )(x, oa, z, z, z, z, z, z, z, z, z, z, z, mod,
      p["conv_w"], p["conv_b"], p["conv_ln_g"], p["conv_ln_b"], p["conv_pw"],
      p["pool_mask"], p["pool_w"], p["pool_scale"],
      p["sgu_ln_g"], p["sgu_ln_b"], p["sgu_w"], p["sgu_b"], p["avg"],
      p["w_out"], p["b_out"], p["ln1_g"], p["ln1_b"],
      p["router_hi"], p["router_lo"], p["router_b"], tri)


def _route(idx, lrank, cnt, tile, rows):
    n = idx.shape[0] * idx.shape[1]
    idx = idx.reshape(n, TOP_K)
    lrank = lrank.reshape(n, TOP_K)
    cnt = cnt.reshape(-1, N_EXPERTS)
    total = jnp.sum(cnt, axis=0)
    padded = ((total + rows - 1) // rows) * rows
    cum_pad = jnp.cumsum(padded)
    base = (cum_pad - padded)[None, :] + jnp.cumsum(cnt, axis=0) - cnt
    base_tok = jnp.repeat(base, tile, axis=0)
    dest = jnp.take_along_axis(base_tok, idx, axis=1) + lrank
    n_blocks = n * TOP_K // rows + N_EXPERTS
    block_start = jnp.arange(n_blocks, dtype=cum_pad.dtype) * rows
    block_e = jnp.minimum(jnp.sum(cum_pad[None, :] <= block_start[:, None], axis=1),
                          N_EXPERTS - 1).astype(jnp.int32)
    return dest.reshape(-1).astype(jnp.int32), block_e, n_blocks


def _dispatch_kernel(tile, dest_ref, h2_ref, xs_in, xs_hbm, sem):
    del xs_in

    def body(n8, carry):
        for j in range(SUBLANES):
            for k in range(TOP_K):
                dst = dest_ref[(n8 * SUBLANES + j) * TOP_K + k]
                pltpu.make_async_copy(h2_ref.at[n8, pl.ds(j, 1)], xs_hbm.at[pl.ds(dst, 1)], sem).start()
        return carry

    lax.fori_loop(0, tile // SUBLANES, body, 0)
    for _ in range(TOP_K):
        pltpu.make_async_copy(xs_hbm.at[pl.ds(0, tile)], xs_hbm.at[pl.ds(0, tile)], sem).wait()


def _dispatch(dest, h2, total_rows, tile):
    n = h2.shape[0]
    xs0 = jnp.zeros((total_rows, D_MODEL), F32)
    return pl.pallas_call(
        functools.partial(_dispatch_kernel, tile),
        grid=(n // tile,),
        in_specs=[pl.BlockSpec((tile * TOP_K,), lambda i: (i,), memory_space=pltpu.SMEM),
                  pl.BlockSpec((tile // SUBLANES, SUBLANES, D_MODEL), lambda i: (i, 0, 0)),
                  pl.BlockSpec(memory_space=pl.ANY)],
        out_specs=pl.BlockSpec(memory_space=pl.ANY),
        out_shape=jax.ShapeDtypeStruct((total_rows, D_MODEL), F32),
        scratch_shapes=[pltpu.SemaphoreType.DMA],
        input_output_aliases={2: 0},
        compiler_params=_params("arbitrary"),
        name="dispatch",
    )(dest, h2.reshape(n // SUBLANES, SUBLANES, D_MODEL), xs0)


def _gu_prep_kernel(w_ref, perm_ref, o_ref):
    o_ref[0, 0] = _dot(w_ref[0, 0].astype(BF16), perm_ref[...]).astype(BF16)


def _gu_prep(w_gu):
    nl, ne, d, c = w_gu.shape
    perm = np.zeros((c, c), np.float32)
    perm[_GU_ORDER, np.arange(c)] = 1.0
    blk = pl.BlockSpec((1, 1, d, c), lambda l, e: (l, e, 0, 0))
    return pl.pallas_call(
        _gu_prep_kernel,
        grid=(nl, ne),
        in_specs=[blk, pl.BlockSpec((c, c), lambda l, e: (0, 0))],
        out_specs=blk,
        out_shape=jax.ShapeDtypeStruct(w_gu.shape, BF16),
        compiler_params=_params("arbitrary", "arbitrary"),
        name="gu_prep",
    )(w_gu, jnp.asarray(perm, BF16))


def _expert_kernel(be_ref, xs_ref, wgu_ref, bgu_ref, wd_ref, bd_ref, y_ref):
    del be_ref
    hu = _dot(xs_ref[...].astype(BF16), wgu_ref[0, 0]) + bgu_ref[0, 0]
    glu = jnp.minimum(hu[:, :D_EXPERT], SWIGLU_LIMIT)
    lin = jnp.clip(hu[:, D_EXPERT:], -SWIGLU_LIMIT, SWIGLU_LIMIT)
    act = glu * jax.nn.sigmoid(SWIGLU_ALPHA * glu) * (lin + 1.0)
    y_ref[...] = _dot(act.astype(BF16), wd_ref[0, 0]) + bd_ref[0, 0]


def _experts(block_e, xs, p, layer, rows, n_blocks):
    def wspec(*shape):
        return pl.BlockSpec((1, 1) + shape, lambda g, be: (layer, be[g]) + (0,) * len(shape))

    grid_spec = pltpu.PrefetchScalarGridSpec(
        num_scalar_prefetch=1,
        grid=(n_blocks,),
        in_specs=[pl.BlockSpec((rows, D_MODEL), lambda g, be: (g, 0)),
                  wspec(D_MODEL, 2 * D_EXPERT), wspec(1, 2 * D_EXPERT),
                  wspec(D_EXPERT, D_MODEL), wspec(1, D_MODEL)],
        out_specs=pl.BlockSpec((rows, D_MODEL), lambda g, be: (g, 0)),
    )
    return pl.pallas_call(
        _expert_kernel,
        grid_spec=grid_spec,
        out_shape=jax.ShapeDtypeStruct(xs.shape, F32),
        compiler_params=_params("arbitrary"),
        name="experts",
    )(block_e, xs, p["w_gu"], p["b_gu"], p["w_dn"], p["b_dn"])


def _combine_kernel(tile, dest_ref, y_hbm, x1_ref, wts_ref, mod_ref, g_ref, b_ref, o_ref, buf, sem):
    t8 = tile // SUBLANES

    def body(n8, carry):
        for j in range(SUBLANES):
            for k in range(TOP_K):
                src = dest_ref[(n8 * SUBLANES + j) * TOP_K + k]
                pltpu.make_async_copy(y_hbm.at[pl.ds(src, 1)], buf.at[k * t8 + n8, pl.ds(j, 1)], sem).start()
        return carry

    lax.fori_loop(0, t8, body, 0)
    for _ in range(TOP_K):
        pltpu.make_async_copy(y_hbm.at[pl.ds(0, tile)], y_hbm.at[pl.ds(0, tile)], sem).wait()
    wts = wts_ref[0]
    f = wts[:, 0:1] * buf[0:t8].reshape(tile, D_MODEL)
    for k in range(1, TOP_K):
        f = f + wts[:, k:k + 1] * buf[k * t8:(k + 1) * t8].reshape(tile, D_MODEL)
    m = mod_ref[0]
    o_ref[0] = _ln(DN_ALPHA * x1_ref[0] + m[5:6] * f) * g_ref[0] + b_ref[0]


def _combine(dest, y, x1, wts, mod, p, layer, tile):
    batch, seq, _ = x1.shape
    nt = seq // tile
    row = pl.BlockSpec((1, tile, D_MODEL), lambda b, i: (b, i, 0))
    vec = pl.BlockSpec((1, 1, D_MODEL), lambda b, i: (layer, 0, 0))
    return pl.pallas_call(
        functools.partial(_combine_kernel, tile),
        grid=(batch, nt),
        in_specs=[pl.BlockSpec((tile * TOP_K,), lambda b, i: (b * nt + i,), memory_space=pltpu.SMEM),
                  pl.BlockSpec(memory_space=pl.ANY),
                  row,
                  pl.BlockSpec((1, tile, TOP_K), lambda b, i: (b, i, 0)),
                  pl.BlockSpec((1, 6, D_MODEL), lambda b, i: (b, 0, 0)),
                  vec, vec],
        out_specs=row,
        out_shape=jax.ShapeDtypeStruct(x1.shape, F32),
        scratch_shapes=[pltpu.VMEM((tile * TOP_K // SUBLANES, SUBLANES, D_MODEL), F32), pltpu.SemaphoreType.DMA],
        compiler_params=_params("arbitrary", "arbitrary"),
        name="combine",
    )(dest, y, x1, wts, mod, p["ln2_g"], p["ln2_b"])


def _prepare(w_in, b_in, fnet_w, conv_w, conv_b, conv_ln_g, conv_ln_b, conv_pw, pool_w, pool_scale,
             sgu_ln_g, sgu_ln_b, sgu_w, sgu_b, w_out, b_out, ln1_g, ln1_b, router_w, router_b,
             w_gu, b_gu, w_dn, b_dn, ln2_g, ln2_b):
    nl = w_in.shape[0]
    vec = lambda t: t.reshape(nl, 1, -1)
    half = np.repeat(np.array(POOL_WINDOWS) // 2, SUBHEAD_DIM)[None, :]
    tap = np.arange(POOL_TAPS)[:, None] - POOL_TAPS // 2
    pool_mask = ((tap >= -half) & (tap < half)).astype(np.float32)
    group = np.arange(GROUP_W) // SUBHEAD_DIM
    avg = (group[:, None] == group[None, :]).astype(np.float32) / SUBHEAD_DIM
    r_hi = router_w.astype(BF16)
    m1, m2 = _fnet_prep(fnet_w)
    return {
        "w_in": w_in.astype(BF16), "b_in": vec(b_in), "fnet_m1": m1, "fnet_m2": m2,
        "conv_w": conv_w, "conv_b": vec(conv_b), "conv_ln_g": vec(conv_ln_g), "conv_ln_b": vec(conv_ln_b),
        "conv_pw": conv_pw.astype(BF16),
        "pool_mask": jnp.asarray(pool_mask), "pool_w": _block_diag(pool_w).astype(BF16),
        "pool_scale": vec(pool_scale),
        "sgu_ln_g": vec(sgu_ln_g), "sgu_ln_b": vec(sgu_ln_b),
        "sgu_w": sgu_w.reshape(nl, N_SUBHEADS * CHUNK, CHUNK).astype(BF16),
        "sgu_b": jnp.repeat(jnp.swapaxes(sgu_b, 1, 2), SUBHEAD_DIM, axis=2),
        "avg": jnp.asarray(avg, BF16),
        "w_out": w_out.astype(BF16), "b_out": vec(b_out), "ln1_g": vec(ln1_g), "ln1_b": vec(ln1_b),
        "router_hi": r_hi, "router_lo": (router_w - r_hi.astype(F32)).astype(BF16), "router_b": vec(router_b),
        "w_gu": _gu_prep(w_gu), "b_gu": b_gu[..., None, _GU_ORDER],
        "w_dn": w_dn.astype(BF16), "b_dn": b_dn[..., None, :],
        "ln2_g": vec(ln2_g), "ln2_b": vec(ln2_b),
    }


def _trunk(x, mod, p):
    batch, seq, _ = x.shape
    t_mix, t_cmb, rows = _tiles(seq)
    tables = _fft_tables(seq)
    for layer in range(DEPTH):
        z = _inproj(x, mod[layer], p["w_in"], p["b_in"], layer, t_mix)
        oa = _fourier(z, batch, seq, tables, p["fnet_m1"], p["fnet_m2"], layer)
        x1, h2, idx, wts, lrank, cnt = _mix(x, z, oa, mod[layer], p, layer, t_mix)
        dest, block_e, n_blocks = _route(idx, lrank, cnt, t_mix, rows)
        xs = _dispatch(dest, h2.reshape(batch * seq, D_MODEL), n_blocks * rows, t_mix)
        y = _experts(block_e, xs, p, layer, rows, n_blocks)
        x = _combine(dest, y, x1, wts, mod[layer], p, layer, t_cmb)
    return x


def kernel(x_prompt, x_sample, c_prompt, c_sample, w_mod, b_mod, w_in, b_in, fnet_w, conv_w, conv_b,
           conv_ln_g, conv_ln_b, conv_pw, pool_w, pool_scale, sgu_ln_g, sgu_ln_b, sgu_w, sgu_b,
           w_out, b_out, ln1_g, ln1_b, router_w, router_b, w_gu, b_gu, w_dn, b_dn, ln2_g, ln2_b):
    p = _prepare(w_in, b_in, fnet_w, conv_w, conv_b, conv_ln_g, conv_ln_b, conv_pw, pool_w, pool_scale,
                 sgu_ln_g, sgu_ln_b, sgu_w, sgu_b, w_out, b_out, ln1_g, ln1_b, router_w, router_b,
                 w_gu, b_gu, w_dn, b_dn, ln2_g, ln2_b)
    nb_p, nb_s = c_prompt.shape[0], c_sample.shape[0]
    pad = (-(nb_p + nb_s)) % 8
    c_all = jnp.concatenate([c_prompt, c_sample, jnp.zeros((pad, D_MODEL), F32)], axis=0)
    mod = _modulation(c_all, w_mod, b_mod)
    nl = mod.shape[0]
    mod_p = mod[:, :nb_p].reshape(nl, nb_p, 6, D_MODEL)
    mod_s = mod[:, nb_p:nb_p + nb_s].reshape(nl, nb_s, 6, D_MODEL)
    return _trunk(x_prompt, mod_p, p), _trunk(x_sample, mod_s, p)
```

```python
import functools
import math

import numpy as np
import jax
import jax.numpy as jnp
from jax import lax
from jax.experimental import pallas as pl
from jax.experimental.pallas import tpu as pltpu

D_MODEL = 1024
DEPTH = 4
GROUP_W = 256
N_SUBHEADS = 4
SUBHEAD_DIM = 64
IN_COLS = 6 * GROUP_W
CONV_WIDTH = 31
POOL_WINDOWS = (2, 4, 8, 16)
CHUNK = 128
N_EXPERTS = 32
TOP_K = 4
D_EXPERT = 512
SWIGLU_LIMIT = 7.0
SWIGLU_ALPHA = 1.702
DN_ALPHA = (2.0 * DEPTH) ** 0.25
LN_EPS = 1e-5

HALO = 16
POOL_TAPS = 16
FFT_ROWS = 8
FFT_COLS = 2048
SUBLANES = 8
VMEM_LIMIT = 48 * 1024 * 1024

F32 = jnp.float32
BF16 = jnp.bfloat16

_GU_ORDER = np.concatenate([np.arange(0, 2 * D_EXPERT, 2), np.arange(1, 2 * D_EXPERT, 2)])


def _tiles(seq):
    tile = min(512, seq)
    return tile, tile


def _dot(a, b):
    return jnp.dot(a, b, preferred_element_type=F32)


def _split(x):
    hi = x.astype(BF16)
    lo = (x - hi.astype(F32)).astype(BF16)
    return hi, lo


def _dot3(a, b_hi, b_lo):
    a_hi, a_lo = _split(a)
    return _dot(a_hi, b_hi) + _dot(a_lo, b_hi) + _dot(a_hi, b_lo)


def _ln(x):
    mu = jnp.mean(x, axis=-1, keepdims=True)
    d = x - mu
    var = jnp.mean(d * d, axis=-1, keepdims=True)
    return d * lax.rsqrt(var + LN_EPS)


def _params(*sem):
    return pltpu.CompilerParams(dimension_semantics=sem, vmem_limit_bytes=VMEM_LIMIT)


def _mod_kernel(c_ref, w_ref, b_ref, o_ref):
    c = c_ref[...]
    s = c * jax.nn.sigmoid(c)
    w_hi, w_lo = _split(w_ref[0])
    o_ref[0] = _dot3(s, w_hi, w_lo) + b_ref[0]


def _modulation(c_all, w_mod, b_mod):
    bp = c_all.shape[0]
    nl = w_mod.shape[0]
    ncol = w_mod.shape[2] // D_MODEL
    return pl.pallas_call(
        _mod_kernel,
        grid=(nl, ncol),
        in_specs=[pl.BlockSpec((bp, D_MODEL), lambda l, j: (0, 0)),
                  pl.BlockSpec((1, D_MODEL, D_MODEL), lambda l, j: (l, 0, j)),
                  pl.BlockSpec((1, 1, D_MODEL), lambda l, j: (l, 0, j))],
        out_specs=pl.BlockSpec((1, bp, D_MODEL), lambda l, j: (l, 0, j)),
        out_shape=jax.ShapeDtypeStruct((nl, bp, w_mod.shape[2]), F32),
        compiler_params=_params("arbitrary", "arbitrary"),
        name="modulation",
    )(c_all, w_mod, b_mod.reshape(nl, 1, -1))


def _fnet_prep_kernel(cc_ref, sc_ref, w_ref, m1_ref, m2_ref):
    w_hi, w_lo = _split(w_ref[0])
    m1_ref[0] = _dot3(cc_ref[...], w_hi, w_lo).astype(BF16)
    m2_ref[0] = _dot3(sc_ref[...], w_hi, w_lo).astype(BF16)


def _block_diag(w):
    nl, h, c, _ = w.shape
    eye = jnp.eye(h, dtype=w.dtype)
    return jnp.einsum("lhcd,hg->lhcgd", w, eye).reshape(nl, h * c, h * c)


def _fnet_prep(fnet_w):
    nl = fnet_w.shape[0]
    k = np.arange(SUBHEAD_DIM)
    ang = 2.0 * np.pi * np.outer(k, k) / SUBHEAD_DIM
    eye = np.eye(N_SUBHEADS)
    cc = jnp.asarray(np.kron(eye, np.cos(ang)), F32)
    sc = jnp.asarray(np.kron(eye, np.sin(ang)), F32)
    wbd = _block_diag(fnet_w)
    mat = pl.BlockSpec((GROUP_W, GROUP_W), lambda l: (0, 0))
    per_layer = pl.BlockSpec((1, GROUP_W, GROUP_W), lambda l: (l, 0, 0))
    return pl.pallas_call(
        _fnet_prep_kernel,
        grid=(nl,),
        in_specs=[mat, mat, per_layer],
        out_specs=[per_layer, per_layer],
        out_shape=[jax.ShapeDtypeStruct((nl, GROUP_W, GROUP_W), BF16)] * 2,
        compiler_params=_params("arbitrary"),
        name="fnet_prep",
    )(cc, sc, wbd)


def _fft_tables(seq):
    n1 = int(round(math.sqrt(seq)))
    n2 = seq // n1
    assert n1 * n2 == seq and n1 % FFT_ROWS == 0 and (n2 * GROUP_W) % FFT_COLS == 0
    scale = 1.0 / math.sqrt(seq * SUBHEAD_DIM)
    a = np.arange(n1)[:, None, None]
    k2 = np.arange(n2)[None, :, None]
    b = np.arange(n2)[None, None, :]
    ang = 2.0 * np.pi * (((a + n1 * b) * k2) % seq) / seq
    g_re = np.cos(ang) * scale
    g_im = -np.sin(ang) * scale
    k1 = np.arange(n1)
    ang2 = 2.0 * np.pi * (np.outer(k1, k1) % n1) / n1
    to = lambda t: jnp.asarray(t, F32).astype(BF16)
    return n1, n2, to(g_re), to(g_im), to(np.cos(ang2)), to(np.sin(ang2))


def _fft1_kernel(z_ref, gr_ref, gi_ref, vr_ref, vi_ref):
    for j in range(FFT_ROWS):
        xs = z_ref[0, :, j, :].astype(BF16)
        vr_ref[0, j] = _dot(gr_ref[j], xs).astype(BF16)
        vi_ref[0, j] = _dot(gi_ref[j], xs).astype(BF16)


def _fft2_kernel(vr_ref, vi_ref, c_ref, s_ref, m1_ref, m2_ref, o_ref):
    vr = vr_ref[0]
    vi = vi_ref[0]
    c = c_ref[...]
    s = s_ref[...]
    yr = _dot(c, vr) + _dot(s, vi)
    yi = _dot(c, vi) - _dot(s, vr)
    m1 = m1_ref[0]
    m2 = m2_ref[0]
    for g in range(FFT_COLS // GROUP_W):
        sl = slice(g * GROUP_W, (g + 1) * GROUP_W)
        o_ref[0, :, sl] = (_dot(yr[:, sl].astype(BF16), m1) + _dot(yi[:, sl].astype(BF16), m2)).astype(BF16)


def _fourier(z, batch, seq, tables, m1, m2, layer):
    n1, n2, g_re, g_im, c2, s2 = tables
    z4 = z.reshape(batch, n2, n1, IN_COLS)
    v_shape = jax.ShapeDtypeStruct((batch, n1, n2, GROUP_W), BF16)
    g_spec = pl.BlockSpec((FFT_ROWS, n2, n2), lambda b, i: (i, 0, 0))
    v_spec = pl.BlockSpec((1, FFT_ROWS, n2, GROUP_W), lambda b, i: (b, i, 0, 0))
    vr, vi = pl.pallas_call(
        _fft1_kernel,
        grid=(batch, n1 // FFT_ROWS),
        in_specs=[pl.BlockSpec((1, n2, FFT_ROWS, GROUP_W), lambda b, i: (b, 0, i, 0)), g_spec, g_spec],
        out_specs=[v_spec, v_spec],
        out_shape=[v_shape, v_shape],
        compiler_params=_params("arbitrary", "arbitrary"),
        name="fft_stage1",
    )(z4, g_re, g_im)
    width = n2 * GROUP_W
    vr = vr.reshape(batch, n1, width)
    vi = vi.reshape(batch, n1, width)
    blk = pl.BlockSpec((1, n1, FFT_COLS), lambda b, j: (b, 0, j))
    tab = pl.BlockSpec((n1, n1), lambda b, j: (0, 0))
    mat = pl.BlockSpec((1, GROUP_W, GROUP_W), lambda b, j: (layer, 0, 0))
    oa = pl.pallas_call(
        _fft2_kernel,
        grid=(batch, width // FFT_COLS),
        in_specs=[blk, blk, tab, tab, mat, mat],
        out_specs=blk,
        out_shape=jax.ShapeDtypeStruct((batch, n1, width), BF16),
        compiler_params=_params("arbitrary", "arbitrary"),
        name="fft_stage2",
    )(vr, vi, c2, s2, m1, m2)
    return oa.reshape(batch, seq, GROUP_W)


def _inproj_kernel(x_ref, mod_ref, w_ref, b_ref, z_ref):
    m = mod_ref[0]
    h = _ln(x_ref[0]) * (1.0 + m[1:2]) + m[0:1]
    z_ref[0] = _dot(h.astype(BF16), w_ref[0]) + b_ref[0]


def _inproj(x, mod, w_in, b_in, layer, tile):
    batch, seq, _ = x.shape
    return pl.pallas_call(
        _inproj_kernel,
        grid=(batch, seq // tile),
        in_specs=[pl.BlockSpec((1, tile, D_MODEL), lambda b, i: (b, i, 0)),
                  pl.BlockSpec((1, 6, D_MODEL), lambda b, i: (b, 0, 0)),
                  pl.BlockSpec((1, D_MODEL, IN_COLS), lambda b, i: (layer, 0, 0)),
                  pl.BlockSpec((1, 1, IN_COLS), lambda b, i: (layer, 0, 0))],
        out_specs=pl.BlockSpec((1, tile, IN_COLS), lambda b, i: (b, i, 0)),
        out_shape=jax.ShapeDtypeStruct((batch, seq, IN_COLS), F32),
        compiler_params=_params("arbitrary", "arbitrary"),
        name="in_proj",
    )(x, mod, w_in, b_in)


def _mix_kernel(seq, tile,
                x_ref, oa_ref,
                bvp_ref, bvm_ref, bvn_ref, bgp_ref, bgm_ref, bgn_ref,
                pcp_ref, pcm_ref, pcn_ref, du_ref, dv_ref, mod_ref,
                convw_ref, convb_ref, clng_ref, clnb_ref, cpw_ref,
                pmask_ref, poolw_ref, pscale_ref,
                slng_ref, slnb_ref, wst_ref, sbias_ref, avg_ref,
                wout_ref, bout_ref, ln1g_ref, ln1b_ref,
                rwh_ref, rwl_ref, rb_ref, tri_ref,
                x1_ref, h2_ref, idx_ref, wts_ref, lrank_ref, cnt_ref,
                gbuf, pbuf):
    i = pl.program_id(1)
    first = i == 0
    last = i == pl.num_programs(1) - 1
    m = mod_ref[0]

    def glu(v_ref, g_ref):
        return v_ref[0] * jax.nn.sigmoid(g_ref[0])

    gbuf[0:HALO] = jnp.where(first, 0.0, glu(bvp_ref, bgp_ref))
    gbuf[HALO:HALO + tile] = glu(bvm_ref, bgm_ref)
    gbuf[HALO + tile:] = jnp.where(last, 0.0, glu(bvn_ref, bgn_ref))
    acc = jnp.zeros((tile, GROUP_W), F32)
    off = HALO - CONV_WIDTH // 2
    for j in range(CONV_WIDTH):
        acc = acc + gbuf[off + j:off + j + tile, :] * convw_ref[0, j:j + 1, :]
    cv = _ln(acc + convb_ref[0]) * clng_ref[0] + clnb_ref[0]
    cv = cv * jax.nn.sigmoid(cv)
    ob = _dot(cv.astype(BF16), cpw_ref[0])

    pc = pcm_ref[0]
    pbuf[0:HALO] = jnp.where(first, 0.0, pcp_ref[0])
    pbuf[HALO:HALO + tile] = pc
    pbuf[HALO + tile:] = jnp.where(last, 0.0, pcn_ref[0])
    ssum = jnp.zeros((tile, GROUP_W), F32)
    off = HALO - POOL_TAPS // 2
    for j in range(POOL_TAPS):
        ssum = ssum + pbuf[off + j:off + j + tile, :] * pmask_ref[j:j + 1, :]
    lane = lax.broadcasted_iota(jnp.int32, (tile, GROUP_W), 1)
    half = jnp.left_shift(1, lane // SUBHEAD_DIM)
    pos = i * tile + lax.broadcasted_iota(jnp.int32, (tile, GROUP_W), 0)
    cnt = jnp.minimum(pos + half, seq) - jnp.maximum(pos - half, 0)
    pooled = ssum / cnt.astype(F32) - pc
    oc = _dot(pooled.astype(BF16), poolw_ref[0]) * pscale_ref[0]

    v = dv_ref[0]
    avg = avg_ref[...]
    v_hi, v_lo = _split(v)
    dv = v - (_dot(v_hi, avg) + _dot(v_lo, avg))
    q_hi, q_lo = _split(dv * dv)
    var = _dot(q_hi, avg) + _dot(q_lo, avg)
    vn = (dv * lax.rsqrt(var + LN_EPS) * slng_ref[0] + slnb_ref[0]).astype(BF16)
    wst = wst_ref[0]
    lane_c = lax.broadcasted_iota(jnp.int32, (CHUNK, GROUP_W), 1) // SUBHEAD_DIM
    mixed = []
    for c in range(tile // CHUNK):
        res = _dot(wst, vn[c * CHUNK:(c + 1) * CHUNK])
        sel = res[0:CHUNK]
        for h in range(1, N_SUBHEADS):
            sel = jnp.where(lane_c == h, res[h * CHUNK:(h + 1) * CHUNK], sel)
        mixed.append(sel + sbias_ref[0])
    od = du_ref[0] * jnp.concatenate(mixed, axis=0)

    g = GROUP_W
    mo = (_dot(oa_ref[0], wout_ref[0, 0:g]) + _dot(ob.astype(BF16), wout_ref[0, g:2 * g])
          + _dot(oc.astype(BF16), wout_ref[0, 2 * g:3 * g]) + _dot(od.astype(BF16), wout_ref[0, 3 * g:4 * g])
          + bout_ref[0])
    x1 = _ln(DN_ALPHA * x_ref[0] + m[2:3] * mo) * ln1g_ref[0] + ln1b_ref[0]
    x1_ref[0] = x1

    h2 = _ln(x1) * (1.0 + m[4:5]) + m[3:4]
    h2_ref[0] = h2
    logits = _dot3(h2, rwh_ref[0], rwl_ref[0]) + rb_ref[0]
    eio = lax.broadcasted_iota(jnp.int32, (tile, N_EXPERTS), 1)
    kio = lax.broadcasted_iota(jnp.int32, (tile, TOP_K), 1)
    work = logits
    picks, vals, hots = [], [], []
    for _ in range(TOP_K):
        mx = jnp.max(work, axis=1, keepdims=True)
        am = jnp.min(jnp.where(work == mx, eio, N_EXPERTS), axis=1, keepdims=True)
        hot = eio == am
        picks.append(am)
        vals.append(mx)
        hots.append(hot)
        work = jnp.where(hot, -jnp.inf, work)
    exps = [jnp.exp(vv - vals[0]) for vv in vals]
    denom = exps[0] + exps[1] + exps[2] + exps[3]
    onehot = jnp.zeros((tile, N_EXPERTS), F32)
    for hot in hots:
        onehot = onehot + hot.astype(F32)
    before = _dot(tri_ref[...], onehot.astype(BF16))
    idx = jnp.zeros((tile, TOP_K), jnp.int32)
    wts = jnp.zeros((tile, TOP_K), F32)
    lrank = jnp.zeros((tile, TOP_K), jnp.int32)
    for k in range(TOP_K):
        rk = jnp.sum(jnp.where(hots[k], before, 0.0), axis=1, keepdims=True).astype(jnp.int32)
        idx = jnp.where(kio == k, picks[k], idx)
        wts = jnp.where(kio == k, exps[k] / denom, wts)
        lrank = jnp.where(kio == k, rk, lrank)
    idx_ref[0] = idx
    wts_ref[0] = wts
    lrank_ref[0] = lrank
    cnt_ref[0, 0] = jnp.sum(onehot, axis=0, keepdims=True).astype(jnp.int32)


def _mix(x, z, oa, mod, p, layer, tile):
    batch, seq, _ = x.shape
    nt = seq // tile
    hb = tile // HALO
    nhb = seq // HALO

    def col(c):
        return pl.BlockSpec((1, tile, GROUP_W), lambda b, i: (b, i, c))

    def prev(c):
        return pl.BlockSpec((1, HALO, GROUP_W), lambda b, i: (b, jnp.maximum(i * hb - 1, 0), c))

    def nxt(c):
        return pl.BlockSpec((1, HALO, GROUP_W), lambda b, i: (b, jnp.minimum((i + 1) * hb, nhb - 1), c))

    def lay(*shape):
        return pl.BlockSpec((1,) + shape, lambda b, i: (layer,) + (0,) * len(shape))

    def const(*shape):
        return pl.BlockSpec(shape, lambda b, i: (0,) * len(shape))

    row = pl.BlockSpec((1, tile, D_MODEL), lambda b, i: (b, i, 0))
    sel = pl.BlockSpec((1, tile, TOP_K), lambda b, i: (b, i, 0))
    in_specs = [
        row, col(0),
        prev(1), col(1), nxt(1), prev(2), col(2), nxt(2),
        prev(3), col(3), nxt(3), col(4), col(5),
        pl.BlockSpec((1, 6, D_MODEL), lambda b, i: (b, 0, 0)),
        lay(CONV_WIDTH, GROUP_W), lay(1, GROUP_W), lay(1, GROUP_W), lay(1, GROUP_W), lay(GROUP_W, GROUP_W),
        const(POOL_TAPS, GROUP_W), lay(GROUP_W, GROUP_W), lay(1, GROUP_W),
        lay(1, GROUP_W), lay(1, GROUP_W), lay(N_SUBHEADS * CHUNK, CHUNK), lay(CHUNK, GROUP_W),
        const(GROUP_W, GROUP_W),
        lay(D_MODEL, D_MODEL), lay(1, D_MODEL), lay(1, D_MODEL), lay(1, D_MODEL),
        lay(D_MODEL, N_EXPERTS), lay(D_MODEL, N_EXPERTS), lay(1, N_EXPERTS), const(tile, tile),
    ]
    out_specs = [row, row, sel, sel, sel,
                 pl.BlockSpec((1, 1, 1, N_EXPERTS), lambda b, i: (b, i, 0, 0))]
    out_shape = [jax.ShapeDtypeStruct((batch, seq, D_MODEL), F32),
                 jax.ShapeDtypeStruct((batch, seq, D_MODEL), F32),
                 jax.ShapeDtypeStruct((batch, seq, TOP_K), jnp.int32),
                 jax.ShapeDtypeStruct((batch, seq, TOP_K), F32),
                 jax.ShapeDtypeStruct((batch, seq, TOP_K), jnp.int32),
                 jax.ShapeDtypeStruct((batch, nt, 1, N_EXPERTS), jnp.int32)]
    tri = jnp.asarray(np.tril(np.ones((tile, tile), np.float32), -1), BF16)
    return pl.pallas_call(
        functools.partial(_mix_kernel, seq, tile),
        grid=(batch, nt),
        in_specs=in_specs,
        out_specs=out_specs,
        out_shape=out_shape,
        scratch_shapes=[pltpu.VMEM((tile + 2 * HALO, GROUP_W), F32),
                        pltpu.VMEM((tile + 2 * HALO, GROUP_W), F32)],
        compiler_params=_params("arbitrary", "arbitrary"),
        name="mix",
    )(x, oa, z, z, z, z, z, z, z, z, z, z, z, mod,
      p["conv_w"], p["conv_b"], p["conv_ln_g"], p["conv_ln_b"], p["conv_pw"],
      p["pool_mask"], p["pool_w"], p["pool_scale"],
      p["sgu_ln_g"], p["sgu_ln_b"], p["sgu_w"], p["sgu_b"], p["avg"],
      p["w_out"], p["b_out"], p["ln1_g"], p["ln1_b"],
      p["router_hi"], p["router_lo"], p["router_b"], tri)


def _sorted_rows(tile):
    return TOP_K * tile + N_EXPERTS * SUBLANES


def _route(idx, lrank, cnt, tile, rows):
    n = idx.shape[0] * idx.shape[1]
    idx = idx.reshape(n, TOP_K)
    lrank = lrank.reshape(n, TOP_K)
    cnt = cnt.reshape(-1, N_EXPERTS)
    n_tiles = cnt.shape[0]
    r8 = rows // SUBLANES
    cnt8 = (cnt + SUBLANES - 1) // SUBLANES
    seg8 = jnp.cumsum(cnt8, axis=1) - cnt8
    tot8 = jnp.sum(cnt8, axis=0)
    pad8 = ((tot8 + r8 - 1) // r8) * r8
    cum8 = jnp.cumsum(pad8)
    start8 = cum8 - pad8
    gb8 = start8[None, :] + jnp.cumsum(cnt8, axis=0) - cnt8
    pos = jnp.take_along_axis(jnp.repeat(seg8 * SUBLANES, tile, axis=0), idx, axis=1) + lrank
    n_blocks = -(-(n * TOP_K + (SUBLANES - 1) * N_EXPERTS * n_tiles) // rows) + N_EXPERTS
    block_start8 = jnp.arange(n_blocks, dtype=cum8.dtype) * r8
    block_e = jnp.minimum(jnp.sum(cum8[None, :] <= block_start8[:, None], axis=1), N_EXPERTS - 1)
    used = (cum8[-1:] // r8)
    zeros = jnp.zeros_like(seg8)
    segs = jnp.stack([seg8, cnt8, gb8, zeros], axis=1).reshape(-1)
    tails = jnp.concatenate([start8 + tot8, pad8 - tot8, used, jnp.zeros((2 * N_EXPERTS - 1,), tot8.dtype)])
    i32 = lambda t: t.astype(jnp.int32)
    return i32(pos), i32(segs), i32(tails), i32(block_e), i32(used), n_blocks


def _segment_chunks(units, bits, visit):
    for b in range(bits):
        @pl.when(((units >> b) & 1) == 1)
        def _():
            visit(((units >> (b + 1)) << (b + 1)) * SUBLANES, SUBLANES << b)


def _segment_copies(seg_ref, bits, tile_buf, sorted_hbm, sem, to_hbm, action):
    def body(e, carry):
        def visit(off, size):
            loc = tile_buf.at[pl.ds(pl.multiple_of(seg_ref[e] * SUBLANES + off, SUBLANES), size)]
            glob = sorted_hbm.at[pl.ds(pl.multiple_of(seg_ref[2 * N_EXPERTS + e] * SUBLANES + off, SUBLANES), size)]
            action(pltpu.make_async_copy(loc, glob, sem) if to_hbm else pltpu.make_async_copy(glob, loc, sem))

        _segment_chunks(seg_ref[N_EXPERTS + e], bits, visit)
        return carry

    lax.fori_loop(0, N_EXPERTS, body, 0)


def _dispatch_kernel(tile, seg_bits, tail_bits, seg_ref, tail_ref, pos_ref, h2_ref, xs_hbm, xbuf, zbuf, sem):
    i = pl.program_id(0)
    q = xbuf.shape[0]
    pos = pos_ref[...]
    lane = lax.broadcasted_iota(jnp.int32, (tile, 128), 1)
    canvas = jnp.full((tile, 128), -1.0, F32)
    for k in range(TOP_K):
        canvas = jnp.where(lane == k, pos[:, k:k + 1].astype(F32), canvas)
    pos_t = canvas.T.astype(jnp.int32)
    qi = lax.broadcasted_iota(jnp.int32, (q, tile), 0)
    hit = qi == pos_t[0:1]
    for k in range(1, TOP_K):
        hit = hit | (qi == pos_t[k:k + 1])
    xbuf[...] = _dot(jnp.where(hit, 1.0, 0.0).astype(BF16), h2_ref[...].astype(BF16))
    _segment_copies(seg_ref, seg_bits, xbuf, xs_hbm, sem, True, lambda cp: cp.start())
    _segment_copies(seg_ref, seg_bits, xbuf, xs_hbm, sem, True, lambda cp: cp.wait())

    @pl.when(i == 0)
    def _():
        zbuf[...] = jnp.zeros(zbuf.shape, F32)

    @pl.when(i == pl.num_programs(0) - 1)
    def _():
        def tails(action):
            def body(e, carry):
                def visit(off, size):
                    dst = xs_hbm.at[pl.ds(pl.multiple_of(tail_ref[e] * SUBLANES + off, SUBLANES), size)]
                    action(pltpu.make_async_copy(zbuf.at[pl.ds(0, size)], dst, sem))

                _segment_chunks(tail_ref[N_EXPERTS + e], tail_bits, visit)
                return carry

            lax.fori_loop(0, N_EXPERTS, body, 0)

        tails(lambda cp: cp.start())
        tails(lambda cp: cp.wait())

        rows = zbuf.shape[0]

        def fill(g, carry):
            cp = pltpu.make_async_copy(zbuf, xs_hbm.at[pl.ds(pl.multiple_of(g * rows, rows), rows)], sem)
            cp.start()
            cp.wait()
            return carry

        lax.fori_loop(tail_ref[2 * N_EXPERTS], xs_hbm.shape[0] // rows, fill, 0)


def _dispatch(segs, tails, pos, h2, total_rows, tile, rows):
    n = h2.shape[0]
    seg_bits = (tile // SUBLANES).bit_length()
    tail_bits = (rows // SUBLANES - 1).bit_length()
    return pl.pallas_call(
        functools.partial(_dispatch_kernel, tile, seg_bits, tail_bits),
        grid=(n // tile,),
        in_specs=[pl.BlockSpec((4 * N_EXPERTS,), lambda i: (i,), memory_space=pltpu.SMEM),
                  pl.BlockSpec((4 * N_EXPERTS,), lambda i: (0,), memory_space=pltpu.SMEM),
                  pl.BlockSpec((tile, TOP_K), lambda i: (i, 0)),
                  pl.BlockSpec((tile, D_MODEL), lambda i: (i, 0))],
        out_specs=pl.BlockSpec(memory_space=pl.ANY),
        out_shape=jax.ShapeDtypeStruct((total_rows, D_MODEL), F32),
        scratch_shapes=[pltpu.VMEM((_sorted_rows(tile), D_MODEL), F32),
                        pltpu.VMEM((rows, D_MODEL), F32),
                        pltpu.SemaphoreType.DMA],
        compiler_params=_params("arbitrary"),
        name="dispatch",
    )(segs, tails, pos, h2)


def _gu_prep_kernel(w_ref, perm_ref, o_ref):
    o_ref[0, 0] = _dot(w_ref[0, 0].astype(BF16), perm_ref[...]).astype(BF16)


def _gu_prep(w_gu):
    nl, ne, d, c = w_gu.shape
    perm = np.zeros((c, c), np.float32)
    perm[_GU_ORDER, np.arange(c)] = 1.0
    blk = pl.BlockSpec((1, 1, d, c), lambda l, e: (l, e, 0, 0))
    return pl.pallas_call(
        _gu_prep_kernel,
        grid=(nl, ne),
        in_specs=[blk, pl.BlockSpec((c, c), lambda l, e: (0, 0))],
        out_specs=blk,
        out_shape=jax.ShapeDtypeStruct(w_gu.shape, BF16),
        compiler_params=_params("arbitrary", "arbitrary"),
        name="gu_prep",
    )(w_gu, jnp.asarray(perm, BF16))


def _expert_kernel(be_ref, used_ref, xs_ref, wgu_ref, bgu_ref, wd_ref, bd_ref, y_ref):
    del be_ref

    @pl.when(pl.program_id(0) >= used_ref[0])
    def _():
        y_ref[...] = jnp.zeros(y_ref.shape, F32)

    @pl.when(pl.program_id(0) < used_ref[0])
    def _():
        hu = _dot(xs_ref[...].astype(BF16), wgu_ref[0, 0]) + bgu_ref[0, 0]
        glu = jnp.minimum(hu[:, :D_EXPERT], SWIGLU_LIMIT)
        lin = jnp.clip(hu[:, D_EXPERT:], -SWIGLU_LIMIT, SWIGLU_LIMIT)
        act = glu * jax.nn.sigmoid(SWIGLU_ALPHA * glu) * (lin + 1.0)
        y_ref[...] = _dot(act.astype(BF16), wd_ref[0, 0]) + bd_ref[0, 0]


def _experts(block_e, used, xs, p, layer, rows, n_blocks):
    def wspec(*shape):
        return pl.BlockSpec((1, 1) + shape, lambda g, be, u: (layer, be[g]) + (0,) * len(shape))

    grid_spec = pltpu.PrefetchScalarGridSpec(
        num_scalar_prefetch=2,
        grid=(n_blocks,),
        in_specs=[pl.BlockSpec((rows, D_MODEL), lambda g, be, u: (g, 0)),
                  wspec(D_MODEL, 2 * D_EXPERT), wspec(1, 2 * D_EXPERT),
                  wspec(D_EXPERT, D_MODEL), wspec(1, D_MODEL)],
        out_specs=pl.BlockSpec((rows, D_MODEL), lambda g, be, u: (g, 0)),
    )
    return pl.pallas_call(
        _expert_kernel,
        grid_spec=grid_spec,
        out_shape=jax.ShapeDtypeStruct(xs.shape, F32),
        compiler_params=_params("arbitrary"),
        name="experts",
    )(block_e, used, xs, p["w_gu"], p["b_gu"], p["w_dn"], p["b_dn"])


def _combine_kernel(tile, seg_bits, seg_ref, y_hbm, pos_ref, x1_ref, wts_ref, mod_ref, g_ref, b_ref, o_ref,
                    ybuf, sem):
    @pl.when((pl.program_id(0) == 0) & (pl.program_id(1) == 0))
    def _():
        ybuf[...] = jnp.zeros(ybuf.shape, F32)

    _segment_copies(seg_ref, seg_bits, ybuf, y_hbm, sem, False, lambda cp: cp.start())
    _segment_copies(seg_ref, seg_bits, ybuf, y_hbm, sem, False, lambda cp: cp.wait())
    q = ybuf.shape[0]
    pos = pos_ref[0]
    wts = wts_ref[0]
    qi = lax.broadcasted_iota(jnp.int32, (tile, q), 1)
    wm = jnp.zeros((tile, q), F32)
    for k in range(TOP_K):
        wm = jnp.where(qi == pos[:, k:k + 1], wts[:, k:k + 1], wm)
    f = _dot(wm.astype(BF16), ybuf[...].astype(BF16))
    m = mod_ref[0]
    o_ref[0] = _ln(DN_ALPHA * x1_ref[0] + m[5:6] * f) * g_ref[0] + b_ref[0]


def _combine(segs, pos, y, x1, wts, mod, p, layer, tile):
    batch, seq, _ = x1.shape
    nt = seq // tile
    row = pl.BlockSpec((1, tile, D_MODEL), lambda b, i: (b, i, 0))
    sel = pl.BlockSpec((1, tile, TOP_K), lambda b, i: (b, i, 0))
    vec = pl.BlockSpec((1, 1, D_MODEL), lambda b, i: (layer, 0, 0))
    return pl.pallas_call(
        functools.partial(_combine_kernel, tile, (tile // SUBLANES).bit_length()),
        grid=(batch, nt),
        in_specs=[pl.BlockSpec((4 * N_EXPERTS,), lambda b, i: (b * nt + i,), memory_space=pltpu.SMEM),
                  pl.BlockSpec(memory_space=pl.ANY),
                  sel, row, sel,
                  pl.BlockSpec((1, 6, D_MODEL), lambda b, i: (b, 0, 0)),
                  vec, vec],
        out_specs=row,
        out_shape=jax.ShapeDtypeStruct(x1.shape, F32),
        scratch_shapes=[pltpu.VMEM((_sorted_rows(tile), D_MODEL), F32), pltpu.SemaphoreType.DMA],
        compiler_params=_params("arbitrary", "arbitrary"),
        name="combine",
    )(segs, y, pos.reshape(batch, seq, TOP_K), x1, wts, mod, p["ln2_g"], p["ln2_b"])


def _prepare(w_in, b_in, fnet_w, conv_w, conv_b, conv_ln_g, conv_ln_b, conv_pw, pool_w, pool_scale,
             sgu_ln_g, sgu_ln_b, sgu_w, sgu_b, w_out, b_out, ln1_g, ln1_b, router_w, router_b,
             w_gu, b_gu, w_dn, b_dn, ln2_g, ln2_b):
    nl = w_in.shape[0]
    vec = lambda t: t.reshape(nl, 1, -1)
    half = np.repeat(np.array(POOL_WINDOWS) // 2, SUBHEAD_DIM)[None, :]
    tap = np.arange(POOL_TAPS)[:, None] - POOL_TAPS // 2
    pool_mask = ((tap >= -half) & (tap < half)).astype(np.float32)
    group = np.arange(GROUP_W) // SUBHEAD_DIM
    avg = (group[:, None] == group[None, :]).astype(np.float32) / SUBHEAD_DIM
    r_hi = router_w.astype(BF16)
    m1, m2 = _fnet_prep(fnet_w)
    return {
        "w_in": w_in.astype(BF16), "b_in": vec(b_in), "fnet_m1": m1, "fnet_m2": m2,
        "conv_w": conv_w, "conv_b": vec(conv_b), "conv_ln_g": vec(conv_ln_g), "conv_ln_b": vec(conv_ln_b),
        "conv_pw": conv_pw.astype(BF16),
        "pool_mask": jnp.asarray(pool_mask), "pool_w": _block_diag(pool_w).astype(BF16),
        "pool_scale": vec(pool_scale),
        "sgu_ln_g": vec(sgu_ln_g), "sgu_ln_b": vec(sgu_ln_b),
        "sgu_w": sgu_w.reshape(nl, N_SUBHEADS * CHUNK, CHUNK).astype(BF16),
        "sgu_b": jnp.repeat(jnp.swapaxes(sgu_b, 1, 2), SUBHEAD_DIM, axis=2),
        "avg": jnp.asarray(avg, BF16),
        "w_out": w_out.astype(BF16), "b_out": vec(b_out), "ln1_g": vec(ln1_g), "ln1_b": vec(ln1_b),
        "router_hi": r_hi, "router_lo": (router_w - r_hi.astype(F32)).astype(BF16), "router_b": vec(router_b),
        "w_gu": _gu_prep(w_gu), "b_gu": b_gu[..., None, _GU_ORDER],
        "w_dn": w_dn.astype(BF16), "b_dn": b_dn[..., None, :],
        "ln2_g": vec(ln2_g), "ln2_b": vec(ln2_b),
    }


def _trunk(x, mod, p):
    batch, seq, _ = x.shape
    tile, rows = _tiles(seq)
    tables = _fft_tables(seq)
    for layer in range(DEPTH):
        z = _inproj(x, mod[layer], p["w_in"], p["b_in"], layer, tile)
        oa = _fourier(z, batch, seq, tables, p["fnet_m1"], p["fnet_m2"], layer)
        x1, h2, idx, wts, lrank, cnt = _mix(x, z, oa, mod[layer], p, layer, tile)
        pos, segs, tails, block_e, used, n_blocks = _route(idx, lrank, cnt, tile, rows)
        xs = _dispatch(segs, tails, pos, h2.reshape(batch * seq, D_MODEL), n_blocks * rows, tile, rows)
        y = _experts(block_e, used, xs, p, layer, rows, n_blocks)
        x = _combine(segs, pos, y, x1, wts, mod[layer], p, layer, tile)
    return x


def kernel(x_prompt, x_sample, c_prompt, c_sample, w_mod, b_mod, w_in, b_in, fnet_w, conv_w, conv_b,
           conv_ln_g, conv_ln_b, conv_pw, pool_w, pool_scale, sgu_ln_g, sgu_ln_b, sgu_w, sgu_b,
           w_out, b_out, ln1_g, ln1_b, router_w, router_b, w_gu, b_gu, w_dn, b_dn, ln2_g, ln2_b):
    p = _prepare(w_in, b_in, fnet_w, conv_w, conv_b, conv_ln_g, conv_ln_b, conv_pw, pool_w, pool_scale,
                 sgu_ln_g, sgu_ln_b, sgu_w, sgu_b, w_out, b_out, ln1_g, ln1_b, router_w, router_b,
                 w_gu, b_gu, w_dn, b_dn, ln2_g, ln2_b)
    nb_p, nb_s = c_prompt.shape[0], c_sample.shape[0]
    pad = (-(nb_p + nb_s)) % 8
    c_all = jnp.concatenate([c_prompt, c_sample, jnp.zeros((pad, D_MODEL), F32)], axis=0)
    mod = _modulation(c_all, w_mod, b_mod)
    nl = mod.shape[0]
    mod_p = mod[:, :nb_p].reshape(nl, nb_p, 6, D_MODEL)
    mod_s = mod[:, nb_p:nb_p + nb_s].reshape(nl, nb_s, 6, D_MODEL)
    return _trunk(x_prompt, mod_p, p), _trunk(x_sample, mod_s, p)
```

```python
import functools
import math

import numpy as np
import jax
import jax.numpy as jnp
from jax import lax
from jax.experimental import pallas as pl
from jax.experimental.pallas import tpu as pltpu

D_MODEL = 1024
DEPTH = 4
GROUP_W = 256
N_SUBHEADS = 4
SUBHEAD_DIM = 64
IN_COLS = 6 * GROUP_W
CONV_WIDTH = 31
POOL_WINDOWS = (2, 4, 8, 16)
CHUNK = 128
N_EXPERTS = 32
TOP_K = 4
D_EXPERT = 512
SWIGLU_LIMIT = 7.0
SWIGLU_ALPHA = 1.702
DN_ALPHA = (2.0 * DEPTH) ** 0.25
LN_EPS = 1e-5

HALO = 16
FFT_ROWS = 8
FFT_COLS = 2048
SUBLANES = 8
VMEM_LIMIT = 48 * 1024 * 1024

F32 = jnp.float32
BF16 = jnp.bfloat16

_GU_ORDER = np.concatenate([np.arange(0, 2 * D_EXPERT, 2), np.arange(1, 2 * D_EXPERT, 2)])


def _tiles(seq):
    tile = min(512, seq)
    return tile, tile


def _dot(a, b):
    return jnp.dot(a, b, preferred_element_type=F32)


def _split(x):
    hi = x.astype(BF16)
    lo = (x - hi.astype(F32)).astype(BF16)
    return hi, lo


def _dot3(a, b_hi, b_lo):
    a_hi, a_lo = _split(a)
    return _dot(a_hi, b_hi) + _dot(a_lo, b_hi) + _dot(a_hi, b_lo)


def _ln(x):
    mu = jnp.mean(x, axis=-1, keepdims=True)
    d = x - mu
    var = jnp.mean(d * d, axis=-1, keepdims=True)
    return d * lax.rsqrt(var + LN_EPS)


def _params(*sem):
    return pltpu.CompilerParams(dimension_semantics=sem, vmem_limit_bytes=VMEM_LIMIT)


def _mod_kernel(c_ref, w_ref, b_ref, o_ref):
    c = c_ref[...]
    s = c * jax.nn.sigmoid(c)
    w_hi, w_lo = _split(w_ref[0])
    o_ref[0] = _dot3(s, w_hi, w_lo) + b_ref[0]


def _modulation(c_all, w_mod, b_mod):
    bp = c_all.shape[0]
    nl = w_mod.shape[0]
    ncol = w_mod.shape[2] // D_MODEL
    return pl.pallas_call(
        _mod_kernel,
        grid=(nl, ncol),
        in_specs=[pl.BlockSpec((bp, D_MODEL), lambda l, j: (0, 0)),
                  pl.BlockSpec((1, D_MODEL, D_MODEL), lambda l, j: (l, 0, j)),
                  pl.BlockSpec((1, 1, D_MODEL), lambda l, j: (l, 0, j))],
        out_specs=pl.BlockSpec((1, bp, D_MODEL), lambda l, j: (l, 0, j)),
        out_shape=jax.ShapeDtypeStruct((nl, bp, w_mod.shape[2]), F32),
        compiler_params=_params("arbitrary", "arbitrary"),
        name="modulation",
    )(c_all, w_mod, b_mod.reshape(nl, 1, -1))


def _fnet_prep_kernel(cc_ref, sc_ref, w_ref, m1_ref, m2_ref):
    w_hi, w_lo = _split(w_ref[0])
    m1_ref[0] = _dot3(cc_ref[...], w_hi, w_lo).astype(BF16)
    m2_ref[0] = _dot3(sc_ref[...], w_hi, w_lo).astype(BF16)


def _block_diag(w):
    nl, h, c, _ = w.shape
    eye = jnp.eye(h, dtype=w.dtype)
    return jnp.einsum("lhcd,hg->lhcgd", w, eye).reshape(nl, h * c, h * c)


def _fnet_prep(fnet_w):
    nl = fnet_w.shape[0]
    k = np.arange(SUBHEAD_DIM)
    ang = 2.0 * np.pi * np.outer(k, k) / SUBHEAD_DIM
    eye = np.eye(N_SUBHEADS)
    cc = jnp.asarray(np.kron(eye, np.cos(ang)), F32)
    sc = jnp.asarray(np.kron(eye, np.sin(ang)), F32)
    wbd = _block_diag(fnet_w)
    mat = pl.BlockSpec((GROUP_W, GROUP_W), lambda l: (0, 0))
    per_layer = pl.BlockSpec((1, GROUP_W, GROUP_W), lambda l: (l, 0, 0))
    return pl.pallas_call(
        _fnet_prep_kernel,
        grid=(nl,),
        in_specs=[mat, mat, per_layer],
        out_specs=[per_layer, per_layer],
        out_shape=[jax.ShapeDtypeStruct((nl, GROUP_W, GROUP_W), BF16)] * 2,
        compiler_params=_params("arbitrary"),
        name="fnet_prep",
    )(cc, sc, wbd)


def _fft_tables(seq):
    n1 = int(round(math.sqrt(seq)))
    n2 = seq // n1
    assert n1 * n2 == seq and n1 % FFT_ROWS == 0 and (n2 * GROUP_W) % FFT_COLS == 0
    scale = 1.0 / math.sqrt(seq * SUBHEAD_DIM)
    a = np.arange(n1)[:, None, None]
    k2 = np.arange(n2)[None, :, None]
    b = np.arange(n2)[None, None, :]
    ang = 2.0 * np.pi * (((a + n1 * b) * k2) % seq) / seq
    g_re = np.cos(ang) * scale
    g_im = -np.sin(ang) * scale
    k1 = np.arange(n1)
    ang2 = 2.0 * np.pi * (np.outer(k1, k1) % n1) / n1
    to = lambda t: jnp.asarray(t, F32).astype(BF16)
    return n1, n2, to(g_re), to(g_im), to(np.cos(ang2)), to(np.sin(ang2))


def _fft1_kernel(z_ref, gr_ref, gi_ref, vr_ref, vi_ref):
    for j in range(FFT_ROWS):
        xs = z_ref[0, :, j, :].astype(BF16)
        vr_ref[0, j] = _dot(gr_ref[j], xs).astype(BF16)
        vi_ref[0, j] = _dot(gi_ref[j], xs).astype(BF16)


def _fft2_kernel(vr_ref, vi_ref, c_ref, s_ref, m1_ref, m2_ref, o_ref):
    vr = vr_ref[0]
    vi = vi_ref[0]
    c = c_ref[...]
    s = s_ref[...]
    yr = _dot(c, vr) + _dot(s, vi)
    yi = _dot(c, vi) - _dot(s, vr)
    m1 = m1_ref[0]
    m2 = m2_ref[0]
    for g in range(FFT_COLS // GROUP_W):
        sl = slice(g * GROUP_W, (g + 1) * GROUP_W)
        o_ref[0, :, sl] = (_dot(yr[:, sl].astype(BF16), m1) + _dot(yi[:, sl].astype(BF16), m2)).astype(BF16)


def _fourier(z, batch, seq, tables, m1, m2, layer):
    n1, n2, g_re, g_im, c2, s2 = tables
    z4 = z.reshape(batch, n2, n1, IN_COLS)
    v_shape = jax.ShapeDtypeStruct((batch, n1, n2, GROUP_W), BF16)
    g_spec = pl.BlockSpec((FFT_ROWS, n2, n2), lambda b, i: (i, 0, 0))
    v_spec = pl.BlockSpec((1, FFT_ROWS, n2, GROUP_W), lambda b, i: (b, i, 0, 0))
    vr, vi = pl.pallas_call(
        _fft1_kernel,
        grid=(batch, n1 // FFT_ROWS),
        in_specs=[pl.BlockSpec((1, n2, FFT_ROWS, GROUP_W), lambda b, i: (b, 0, i, 0)), g_spec, g_spec],
        out_specs=[v_spec, v_spec],
        out_shape=[v_shape, v_shape],
        compiler_params=_params("arbitrary", "arbitrary"),
        name="fft_stage1",
    )(z4, g_re, g_im)
    width = n2 * GROUP_W
    vr = vr.reshape(batch, n1, width)
    vi = vi.reshape(batch, n1, width)
    blk = pl.BlockSpec((1, n1, FFT_COLS), lambda b, j: (b, 0, j))
    tab = pl.BlockSpec((n1, n1), lambda b, j: (0, 0))
    mat = pl.BlockSpec((1, GROUP_W, GROUP_W), lambda b, j: (layer, 0, 0))
    oa = pl.pallas_call(
        _fft2_kernel,
        grid=(batch, width // FFT_COLS),
        in_specs=[blk, blk, tab, tab, mat, mat],
        out_specs=blk,
        out_shape=jax.ShapeDtypeStruct((batch, n1, width), BF16),
        compiler_params=_params("arbitrary", "arbitrary"),
        name="fft_stage2",
    )(vr, vi, c2, s2, m1, m2)
    return oa.reshape(batch, seq, GROUP_W)


def _inproj_kernel(x_ref, mod_ref, w_ref, b_ref, z_ref):
    m = mod_ref[0]
    h = _ln(x_ref[0]) * (1.0 + m[1:2]) + m[0:1]
    z_ref[0] = _dot(h.astype(BF16), w_ref[0]) + b_ref[0]


def _inproj(x, mod, w_in, b_in, layer, tile):
    batch, seq, _ = x.shape
    return pl.pallas_call(
        _inproj_kernel,
        grid=(batch, seq // tile),
        in_specs=[pl.BlockSpec((1, tile, D_MODEL), lambda b, i: (b, i, 0)),
                  pl.BlockSpec((1, 6, D_MODEL), lambda b, i: (b, 0, 0)),
                  pl.BlockSpec((1, D_MODEL, IN_COLS), lambda b, i: (layer, 0, 0)),
                  pl.BlockSpec((1, 1, IN_COLS), lambda b, i: (layer, 0, 0))],
        out_specs=pl.BlockSpec((1, tile, IN_COLS), lambda b, i: (b, i, 0)),
        out_shape=jax.ShapeDtypeStruct((batch, seq, IN_COLS), F32),
        compiler_params=_params("arbitrary", "arbitrary"),
        name="in_proj",
    )(x, mod, w_in, b_in)


def _mix_kernel(seq, tile,
                x_ref, oa_ref,
                bvp_ref, bvm_ref, bvn_ref, bgp_ref, bgm_ref, bgn_ref,
                pcp_ref, pcm_ref, pcn_ref, du_ref, dv_ref, mod_ref,
                convw_ref, convb_ref, clng_ref, clnb_ref, cpw_ref,
                poolw_ref, pscale_ref,
                slng_ref, slnb_ref, wst_ref, sbias_ref, avg_ref,
                wout_ref, bout_ref, ln1g_ref, ln1b_ref,
                rwh_ref, rwl_ref, rb_ref, tri_ref, upper_ref,
                x1_ref, h2_ref, pos_ref, wts_ref, cnt_ref):
    i = pl.program_id(1)
    first = i == 0
    last = i == pl.num_programs(1) - 1
    m = mod_ref[0]

    def glu(v_ref, g_ref):
        return v_ref[0] * jax.nn.sigmoid(g_ref[0])

    full = tile + 2 * HALO
    gfull = jnp.concatenate([jnp.where(first, 0.0, glu(bvp_ref, bgp_ref)), glu(bvm_ref, bgm_ref),
                             jnp.where(last, 0.0, glu(bvn_ref, bgn_ref))], axis=0)
    acc = jnp.zeros((tile, GROUP_W), F32)
    off = HALO - CONV_WIDTH // 2
    for r in range(SUBLANES):
        shifted = pltpu.roll(gfull, full - (off + r), 0)
        for j in range(r, CONV_WIDTH, SUBLANES):
            acc = acc + shifted[j - r:j - r + tile] * convw_ref[0, j:j + 1, :]
    cv = _ln(acc + convb_ref[0]) * clng_ref[0] + clnb_ref[0]
    cv = cv * jax.nn.sigmoid(cv)
    ob = _dot(cv.astype(BF16), cpw_ref[0])

    pc = pcm_ref[0]
    win = jnp.concatenate([jnp.where(first, 0.0, pcp_ref[0]), pc, jnp.where(last, 0.0, pcn_ref[0])], axis=0)
    win = win + pltpu.roll(win, 1, 0)
    sums = [win]
    for step in (1, 2, 4):
        win = pltpu.roll(win, step, 0) + pltpu.roll(win, full - step, 0)
        sums.append(win)
    lane = lax.broadcasted_iota(jnp.int32, (tile, GROUP_W), 1)
    group = lane // SUBHEAD_DIM
    ssum = sums[0][HALO:HALO + tile]
    for gi in range(1, len(POOL_WINDOWS)):
        ssum = jnp.where(group == gi, sums[gi][HALO:HALO + tile], ssum)
    half = jnp.left_shift(1, group)
    pos = i * tile + lax.broadcasted_iota(jnp.int32, (tile, GROUP_W), 0)
    cnt = jnp.minimum(pos + half, seq) - jnp.maximum(pos - half, 0)
    pooled = ssum / cnt.astype(F32) - pc
    oc = _dot(pooled.astype(BF16), poolw_ref[0]) * pscale_ref[0]

    v = dv_ref[0]
    avg = avg_ref[...]
    v_hi, v_lo = _split(v)
    dv = v - (_dot(v_hi, avg) + _dot(v_lo, avg))
    q_hi, q_lo = _split(dv * dv)
    var = _dot(q_hi, avg) + _dot(q_lo, avg)
    vn = (dv * lax.rsqrt(var + LN_EPS) * slng_ref[0] + slnb_ref[0]).astype(BF16)
    wst = wst_ref[0]
    lane_c = lax.broadcasted_iota(jnp.int32, (CHUNK, GROUP_W), 1) // SUBHEAD_DIM
    mixed = []
    for c in range(tile // CHUNK):
        res = _dot(wst, vn[c * CHUNK:(c + 1) * CHUNK])
        sel = res[0:CHUNK]
        for h in range(1, N_SUBHEADS):
            sel = jnp.where(lane_c == h, res[h * CHUNK:(h + 1) * CHUNK], sel)
        mixed.append(sel + sbias_ref[0])
    od = du_ref[0] * jnp.concatenate(mixed, axis=0)

    g = GROUP_W
    mo = (_dot(oa_ref[0], wout_ref[0, 0:g]) + _dot(ob.astype(BF16), wout_ref[0, g:2 * g])
          + _dot(oc.astype(BF16), wout_ref[0, 2 * g:3 * g]) + _dot(od.astype(BF16), wout_ref[0, 3 * g:4 * g])
          + bout_ref[0])
    x1 = _ln(DN_ALPHA * x_ref[0] + m[2:3] * mo) * ln1g_ref[0] + ln1b_ref[0]
    x1_ref[0] = x1

    h2 = _ln(x1) * (1.0 + m[4:5]) + m[3:4]
    h2_ref[0] = h2
    logits = _dot3(h2, rwh_ref[0], rwl_ref[0]) + rb_ref[0]
    eio = lax.broadcasted_iota(jnp.int32, (tile, N_EXPERTS), 1)
    kio = lax.broadcasted_iota(jnp.int32, (tile, TOP_K), 1)
    work = logits
    vals, hots = [], []
    for _ in range(TOP_K):
        mx = jnp.max(work, axis=1, keepdims=True)
        am = jnp.min(jnp.where(work == mx, eio, N_EXPERTS), axis=1, keepdims=True)
        hot = eio == am
        vals.append(mx)
        hots.append(hot)
        work = jnp.where(hot, -jnp.inf, work)
    exps = [jnp.exp(vv - vals[0]) for vv in vals]
    denom = exps[0] + exps[1] + exps[2] + exps[3]
    onehot = jnp.zeros((tile, N_EXPERTS), F32)
    for hot in hots:
        onehot = onehot + hot.astype(F32)
    before = _dot(tri_ref[...], onehot.astype(BF16))
    cnt = jnp.sum(onehot, axis=0, keepdims=True).astype(jnp.int32)
    cnt8 = jnp.broadcast_to(jnp.right_shift(cnt + (SUBLANES - 1), 3), (SUBLANES, N_EXPERTS))
    seg = _dot(cnt8.astype(BF16), upper_ref[...])[0:1] * float(SUBLANES)
    where_to = seg + before
    pos = jnp.zeros((tile, TOP_K), jnp.int32)
    wts = jnp.zeros((tile, TOP_K), F32)
    for k in range(TOP_K):
        pk = jnp.sum(jnp.where(hots[k], where_to, 0.0), axis=1, keepdims=True).astype(jnp.int32)
        pos = jnp.where(kio == k, pk, pos)
        wts = jnp.where(kio == k, exps[k] / denom, wts)
    pos_ref[...] = pos
    wts_ref[...] = wts
    cnt_ref[0] = cnt


def _mix(x, z, oa, mod, p, layer, tile):
    batch, seq, _ = x.shape
    nt = seq // tile
    hb = tile // HALO
    nhb = seq // HALO

    def col(c):
        return pl.BlockSpec((1, tile, GROUP_W), lambda b, i: (b, i, c))

    def prev(c):
        return pl.BlockSpec((1, HALO, GROUP_W), lambda b, i: (b, jnp.maximum(i * hb - 1, 0), c))

    def nxt(c):
        return pl.BlockSpec((1, HALO, GROUP_W), lambda b, i: (b, jnp.minimum((i + 1) * hb, nhb - 1), c))

    def lay(*shape):
        return pl.BlockSpec((1,) + shape, lambda b, i: (layer,) + (0,) * len(shape))

    def const(*shape):
        return pl.BlockSpec(shape, lambda b, i: (0,) * len(shape))

    row = pl.BlockSpec((1, tile, D_MODEL), lambda b, i: (b, i, 0))
    sel = pl.BlockSpec((tile, TOP_K), lambda b, i: (b * nt + i, 0))
    in_specs = [
        row, col(0),
        prev(1), col(1), nxt(1), prev(2), col(2), nxt(2),
        prev(3), col(3), nxt(3), col(4), col(5),
        pl.BlockSpec((1, 6, D_MODEL), lambda b, i: (b, 0, 0)),
        lay(CONV_WIDTH, GROUP_W), lay(1, GROUP_W), lay(1, GROUP_W), lay(1, GROUP_W), lay(GROUP_W, GROUP_W),
        lay(GROUP_W, GROUP_W), lay(1, GROUP_W),
        lay(1, GROUP_W), lay(1, GROUP_W), lay(N_SUBHEADS * CHUNK, CHUNK), lay(CHUNK, GROUP_W),
        const(GROUP_W, GROUP_W),
        lay(D_MODEL, D_MODEL), lay(1, D_MODEL), lay(1, D_MODEL), lay(1, D_MODEL),
        lay(D_MODEL, N_EXPERTS), lay(D_MODEL, N_EXPERTS), lay(1, N_EXPERTS),
        const(tile, tile), const(N_EXPERTS, N_EXPERTS),
    ]
    out_specs = [row, row, sel, sel,
                 pl.BlockSpec((1, 1, N_EXPERTS), lambda b, i: (b * nt + i, 0, 0))]
    out_shape = [jax.ShapeDtypeStruct((batch, seq, D_MODEL), F32),
                 jax.ShapeDtypeStruct((batch, seq, D_MODEL), F32),
                 jax.ShapeDtypeStruct((batch * seq, TOP_K), jnp.int32),
                 jax.ShapeDtypeStruct((batch * seq, TOP_K), F32),
                 jax.ShapeDtypeStruct((batch * nt, 1, N_EXPERTS), jnp.int32)]
    tri = jnp.asarray(np.tril(np.ones((tile, tile), np.float32), -1), BF16)
    upper = jnp.asarray(np.triu(np.ones((N_EXPERTS, N_EXPERTS), np.float32), 1), BF16)
    return pl.pallas_call(
        functools.partial(_mix_kernel, seq, tile),
        grid=(batch, nt),
        in_specs=in_specs,
        out_specs=out_specs,
        out_shape=out_shape,
        compiler_params=_params("arbitrary", "arbitrary"),
        name="mix",
    )(x, oa, z, z, z, z, z, z, z, z, z, z, z, mod,
      p["conv_w"], p["conv_b"], p["conv_ln_g"], p["conv_ln_b"], p["conv_pw"],
      p["pool_w"], p["pool_scale"],
      p["sgu_ln_g"], p["sgu_ln_b"], p["sgu_w"], p["sgu_b"], p["avg"],
      p["w_out"], p["b_out"], p["ln1_g"], p["ln1_b"],
      p["router_hi"], p["router_lo"], p["router_b"], tri, upper)


def _sorted_rows(tile):
    return TOP_K * tile + N_EXPERTS * SUBLANES


def _route(cnt, tile, rows):
    cnt = cnt.reshape(-1, N_EXPERTS)
    n_tiles = cnt.shape[0]
    r8 = rows // SUBLANES
    cnt8 = (cnt + SUBLANES - 1) // SUBLANES
    seg8 = jnp.cumsum(cnt8, axis=1) - cnt8
    tot8 = jnp.sum(cnt8, axis=0)
    pad8 = ((tot8 + r8 - 1) // r8) * r8
    cum8 = jnp.cumsum(pad8)
    start8 = cum8 - pad8
    gb8 = start8[None, :] + jnp.cumsum(cnt8, axis=0) - cnt8
    n_blocks = -(-(n_tiles * tile * TOP_K + (SUBLANES - 1) * N_EXPERTS * n_tiles) // rows) + N_EXPERTS
    block_start8 = jnp.arange(n_blocks, dtype=cum8.dtype) * r8
    block_e = jnp.minimum(jnp.sum(cum8[None, :] <= block_start8[:, None], axis=1), N_EXPERTS - 1)
    used = (cum8[-1:] // r8)
    zeros = jnp.zeros_like(seg8)
    segs = jnp.stack([seg8, cnt8, gb8, zeros], axis=1).reshape(-1)
    tails = jnp.concatenate([start8 + tot8, pad8 - tot8, used, jnp.zeros((2 * N_EXPERTS - 1,), tot8.dtype)])
    i32 = lambda t: t.astype(jnp.int32)
    return i32(segs), i32(tails), i32(block_e), i32(used), n_blocks


def _segment_chunks(units, bits, visit):
    for b in range(bits):
        @pl.when(((units >> b) & 1) == 1)
        def _():
            visit(((units >> (b + 1)) << (b + 1)) * SUBLANES, SUBLANES << b)


def _segment_copies(seg_ref, bits, tile_buf, sorted_hbm, sem, to_hbm, action):
    def body(e, carry):
        def visit(off, size):
            loc = tile_buf.at[pl.ds(pl.multiple_of(seg_ref[e] * SUBLANES + off, SUBLANES), size)]
            glob = sorted_hbm.at[pl.ds(pl.multiple_of(seg_ref[2 * N_EXPERTS + e] * SUBLANES + off, SUBLANES), size)]
            action(pltpu.make_async_copy(loc, glob, sem) if to_hbm else pltpu.make_async_copy(glob, loc, sem))

        _segment_chunks(seg_ref[N_EXPERTS + e], bits, visit)
        return carry

    lax.fori_loop(0, N_EXPERTS, body, 0)


def _sorted_chunks(q):
    n = next(n for n in (3, 2, 1) if (q // 128) % n == 0)
    return [(c * (q // n), q // n) for c in range(n)]


def _dispatch_kernel(tile, seg_bits, tail_bits, seg_ref, prev_ref, tail_ref, pos_ref, h2_ref, xs_hbm,
                     xbuf, zbuf, sems):
    i = pl.program_id(0)
    slot = i % 2
    q = xbuf.shape[1]
    pos = pos_ref[...]
    lane = lax.broadcasted_iota(jnp.int32, (tile, 128), 1)
    canvas = jnp.full((tile, 128), -1.0, F32)
    for k in range(TOP_K):
        canvas = jnp.where(lane == k, pos[:, k:k + 1].astype(F32), canvas)
    pos_t = canvas.T.astype(jnp.int32)
    h2b = h2_ref[...].astype(BF16)
    for start, size in _sorted_chunks(q):
        qi = start + lax.broadcasted_iota(jnp.int32, (size, tile), 0)
        hit = qi == pos_t[0:1]
        for k in range(1, TOP_K):
            hit = hit | (qi == pos_t[k:k + 1])
        xbuf[slot, start:start + size] = _dot(jnp.where(hit, 1.0, 0.0).astype(BF16), h2b)

    @pl.when(i > 0)
    def _():
        _segment_copies(prev_ref, seg_bits, xbuf.at[1 - slot], xs_hbm, sems.at[1 - slot], True, lambda cp: cp.wait())

    _segment_copies(seg_ref, seg_bits, xbuf.at[slot], xs_hbm, sems.at[slot], True, lambda cp: cp.start())

    @pl.when(i == 0)
    def _():
        zbuf[...] = jnp.zeros(zbuf.shape, F32)

    @pl.when(i == pl.num_programs(0) - 1)
    def _():
        _segment_copies(seg_ref, seg_bits, xbuf.at[slot], xs_hbm, sems.at[slot], True, lambda cp: cp.wait())
        sem = sems.at[0]

        def tails(action):
            def body(e, carry):
                def visit(off, size):
                    dst = xs_hbm.at[pl.ds(pl.multiple_of(tail_ref[e] * SUBLANES + off, SUBLANES), size)]
                    action(pltpu.make_async_copy(zbuf.at[pl.ds(0, size)], dst, sem))

                _segment_chunks(tail_ref[N_EXPERTS + e], tail_bits, visit)
                return carry

            lax.fori_loop(0, N_EXPERTS, body, 0)

        tails(lambda cp: cp.start())
        tails(lambda cp: cp.wait())

        rows = zbuf.shape[0]

        def fill(g, carry):
            cp = pltpu.make_async_copy(zbuf, xs_hbm.at[pl.ds(pl.multiple_of(g * rows, rows), rows)], sem)
            cp.start()
            cp.wait()
            return carry

        lax.fori_loop(tail_ref[2 * N_EXPERTS], xs_hbm.shape[0] // rows, fill, 0)


def _dispatch(segs, tails, pos, h2, total_rows, tile, rows):
    n = h2.shape[0]
    seg_bits = (tile // SUBLANES).bit_length()
    tail_bits = (rows // SUBLANES - 1).bit_length()
    return pl.pallas_call(
        functools.partial(_dispatch_kernel, tile, seg_bits, tail_bits),
        grid=(n // tile,),
        in_specs=[pl.BlockSpec((4 * N_EXPERTS,), lambda i: (i,), memory_space=pltpu.SMEM),
                  pl.BlockSpec((4 * N_EXPERTS,), lambda i: (jnp.maximum(i - 1, 0),), memory_space=pltpu.SMEM),
                  pl.BlockSpec((4 * N_EXPERTS,), lambda i: (0,), memory_space=pltpu.SMEM),
                  pl.BlockSpec((tile, TOP_K), lambda i: (i, 0)),
                  pl.BlockSpec((tile, D_MODEL), lambda i: (i, 0))],
        out_specs=pl.BlockSpec(memory_space=pl.ANY),
        out_shape=jax.ShapeDtypeStruct((total_rows, D_MODEL), F32),
        scratch_shapes=[pltpu.VMEM((2, _sorted_rows(tile), D_MODEL), F32),
                        pltpu.VMEM((rows, D_MODEL), F32),
                        pltpu.SemaphoreType.DMA((2,))],
        compiler_params=_params("arbitrary"),
        name="dispatch",
    )(segs, segs, tails, pos, h2)


def _gu_prep_kernel(w_ref, perm_ref, o_ref):
    o_ref[0, 0] = _dot(w_ref[0, 0].astype(BF16), perm_ref[...]).astype(BF16)


def _gu_prep(w_gu):
    nl, ne, d, c = w_gu.shape
    perm = np.zeros((c, c), np.float32)
    perm[_GU_ORDER, np.arange(c)] = 1.0
    blk = pl.BlockSpec((1, 1, d, c), lambda l, e: (l, e, 0, 0))
    return pl.pallas_call(
        _gu_prep_kernel,
        grid=(nl, ne),
        in_specs=[blk, pl.BlockSpec((c, c), lambda l, e: (0, 0))],
        out_specs=blk,
        out_shape=jax.ShapeDtypeStruct(w_gu.shape, BF16),
        compiler_params=_params("arbitrary", "arbitrary"),
        name="gu_prep",
    )(w_gu, jnp.asarray(perm, BF16))


def _expert_kernel(be_ref, used_ref, xs_ref, wgu_ref, bgu_ref, wd_ref, bd_ref, y_ref):
    del be_ref

    @pl.when(pl.program_id(0) >= used_ref[0])
    def _():
        y_ref[...] = jnp.zeros(y_ref.shape, F32)

    @pl.when(pl.program_id(0) < used_ref[0])
    def _():
        hu = _dot(xs_ref[...].astype(BF16), wgu_ref[0, 0]) + bgu_ref[0, 0]
        glu = jnp.minimum(hu[:, :D_EXPERT], SWIGLU_LIMIT)
        lin = jnp.clip(hu[:, D_EXPERT:], -SWIGLU_LIMIT, SWIGLU_LIMIT)
        act = glu * jax.nn.sigmoid(SWIGLU_ALPHA * glu) * (lin + 1.0)
        y_ref[...] = _dot(act.astype(BF16), wd_ref[0, 0]) + bd_ref[0, 0]


def _experts(block_e, used, xs, p, layer, rows, n_blocks):
    def wspec(*shape):
        return pl.BlockSpec((1, 1) + shape, lambda g, be, u: (layer, be[g]) + (0,) * len(shape))

    grid_spec = pltpu.PrefetchScalarGridSpec(
        num_scalar_prefetch=2,
        grid=(n_blocks,),
        in_specs=[pl.BlockSpec((rows, D_MODEL), lambda g, be, u: (g, 0)),
                  wspec(D_MODEL, 2 * D_EXPERT), wspec(1, 2 * D_EXPERT),
                  wspec(D_EXPERT, D_MODEL), wspec(1, D_MODEL)],
        out_specs=pl.BlockSpec((rows, D_MODEL), lambda g, be, u: (g, 0)),
    )
    return pl.pallas_call(
        _expert_kernel,
        grid_spec=grid_spec,
        out_shape=jax.ShapeDtypeStruct(xs.shape, F32),
        compiler_params=_params("arbitrary"),
        name="experts",
    )(block_e, used, xs, p["w_gu"], p["b_gu"], p["w_dn"], p["b_dn"])


def _combine_kernel(tile, seg_bits, seg_ref, next_ref, y_hbm, pos_ref, x1_ref, wts_ref, mod_ref, g_ref, b_ref,
                    o_ref, ybuf, sems):
    t = pl.program_id(0) * pl.num_programs(1) + pl.program_id(1)
    n_tiles = pl.num_programs(0) * pl.num_programs(1)
    slot = t % 2

    @pl.when(t == 0)
    def _():
        ybuf[...] = jnp.zeros(ybuf.shape, F32)
        _segment_copies(seg_ref, seg_bits, ybuf.at[0], y_hbm, sems.at[0], False, lambda cp: cp.start())

    @pl.when(t + 1 < n_tiles)
    def _():
        _segment_copies(next_ref, seg_bits, ybuf.at[1 - slot], y_hbm, sems.at[1 - slot], False,
                        lambda cp: cp.start())

    _segment_copies(seg_ref, seg_bits, ybuf.at[slot], y_hbm, sems.at[slot], False, lambda cp: cp.wait())
    pos = pos_ref[...]
    wts = wts_ref[...]
    f = jnp.zeros((tile, D_MODEL), F32)
    for start, size in _sorted_chunks(ybuf.shape[1]):
        qi = start + lax.broadcasted_iota(jnp.int32, (tile, size), 1)
        wm = jnp.zeros((tile, size), F32)
        for k in range(TOP_K):
            wm = jnp.where(qi == pos[:, k:k + 1], wts[:, k:k + 1], wm)
        f = f + _dot(wm.astype(BF16), ybuf[slot, start:start + size].astype(BF16))
    m = mod_ref[0]
    o_ref[0] = _ln(DN_ALPHA * x1_ref[0] + m[5:6] * f) * g_ref[0] + b_ref[0]


def _combine(segs, pos, y, x1, wts, mod, p, layer, tile):
    batch, seq, _ = x1.shape
    nt = seq // tile
    last = batch * nt - 1
    row = pl.BlockSpec((1, tile, D_MODEL), lambda b, i: (b, i, 0))
    sel = pl.BlockSpec((tile, TOP_K), lambda b, i: (b * nt + i, 0))
    vec = pl.BlockSpec((1, 1, D_MODEL), lambda b, i: (layer, 0, 0))
    return pl.pallas_call(
        functools.partial(_combine_kernel, tile, (tile // SUBLANES).bit_length()),
        grid=(batch, nt),
        in_specs=[pl.BlockSpec((4 * N_EXPERTS,), lambda b, i: (b * nt + i,), memory_space=pltpu.SMEM),
                  pl.BlockSpec((4 * N_EXPERTS,), lambda b, i: (jnp.minimum(b * nt + i + 1, last),),
                               memory_space=pltpu.SMEM),
                  pl.BlockSpec(memory_space=pl.ANY),
                  sel, row, sel,
                  pl.BlockSpec((1, 6, D_MODEL), lambda b, i: (b, 0, 0)),
                  vec, vec],
        out_specs=row,
        out_shape=jax.ShapeDtypeStruct(x1.shape, F32),
        scratch_shapes=[pltpu.VMEM((2, _sorted_rows(tile), D_MODEL), F32), pltpu.SemaphoreType.DMA((2,))],
        compiler_params=_params("arbitrary", "arbitrary"),
        name="combine",
    )(segs, segs, y, pos, x1, wts, mod, p["ln2_g"], p["ln2_b"])


def _prepare(w_in, b_in, fnet_w, conv_w, conv_b, conv_ln_g, conv_ln_b, conv_pw, pool_w, pool_scale,
             sgu_ln_g, sgu_ln_b, sgu_w, sgu_b, w_out, b_out, ln1_g, ln1_b, router_w, router_b,
             w_gu, b_gu, w_dn, b_dn, ln2_g, ln2_b):
    nl = w_in.shape[0]
    vec = lambda t: t.reshape(nl, 1, -1)
    group = np.arange(GROUP_W) // SUBHEAD_DIM
    avg = (group[:, None] == group[None, :]).astype(np.float32) / SUBHEAD_DIM
    r_hi = router_w.astype(BF16)
    m1, m2 = _fnet_prep(fnet_w)
    return {
        "w_in": w_in.astype(BF16), "b_in": vec(b_in), "fnet_m1": m1, "fnet_m2": m2,
        "conv_w": conv_w, "conv_b": vec(conv_b), "conv_ln_g": vec(conv_ln_g), "conv_ln_b": vec(conv_ln_b),
        "conv_pw": conv_pw.astype(BF16),
        "pool_w": _block_diag(pool_w).astype(BF16),
        "pool_scale": vec(pool_scale),
        "sgu_ln_g": vec(sgu_ln_g), "sgu_ln_b": vec(sgu_ln_b),
        "sgu_w": sgu_w.reshape(nl, N_SUBHEADS * CHUNK, CHUNK).astype(BF16),
        "sgu_b": jnp.repeat(jnp.swapaxes(sgu_b, 1, 2), SUBHEAD_DIM, axis=2),
        "avg": jnp.asarray(avg, BF16),
        "w_out": w_out.astype(BF16), "b_out": vec(b_out), "ln1_g": vec(ln1_g), "ln1_b": vec(ln1_b),
        "router_hi": r_hi, "router_lo": (router_w - r_hi.astype(F32)).astype(BF16), "router_b": vec(router_b),
        "w_gu": _gu_prep(w_gu), "b_gu": b_gu[..., None, _GU_ORDER],
        "w_dn": w_dn.astype(BF16), "b_dn": b_dn[..., None, :],
        "ln2_g": vec(ln2_g), "ln2_b": vec(ln2_b),
    }


def _trunk(x, mod, p):
    batch, seq, _ = x.shape
    tile, rows = _tiles(seq)
    tables = _fft_tables(seq)
    for layer in range(DEPTH):
        z = _inproj(x, mod[layer], p["w_in"], p["b_in"], layer, tile)
        oa = _fourier(z, batch, seq, tables, p["fnet_m1"], p["fnet_m2"], layer)
        x1, h2, pos, wts, cnt = _mix(x, z, oa, mod[layer], p, layer, tile)
        segs, tails, block_e, used, n_blocks = _route(cnt, tile, rows)
        xs = _dispatch(segs, tails, pos, h2.reshape(batch * seq, D_MODEL), n_blocks * rows, tile, rows)
        y = _experts(block_e, used, xs, p, layer, rows, n_blocks)
        x = _combine(segs, pos, y, x1, wts, mod[layer], p, layer, tile)
    return x


def kernel(x_prompt, x_sample, c_prompt, c_sample, w_mod, b_mod, w_in, b_in, fnet_w, conv_w, conv_b,
           conv_ln_g, conv_ln_b, conv_pw, pool_w, pool_scale, sgu_ln_g, sgu_ln_b, sgu_w, sgu_b,
           w_out, b_out, ln1_g, ln1_b, router_w, router_b, w_gu, b_gu, w_dn, b_dn, ln2_g, ln2_b):
    p = _prepare(w_in, b_in, fnet_w, conv_w, conv_b, conv_ln_g, conv_ln_b, conv_pw, pool_w, pool_scale,
                 sgu_ln_g, sgu_ln_b, sgu_w, sgu_b, w_out, b_out, ln1_g, ln1_b, router_w, router_b,
                 w_gu, b_gu, w_dn, b_dn, ln2_g, ln2_b)
    nb_p, nb_s = c_prompt.shape[0], c_sample.shape[0]
    pad = (-(nb_p + nb_s)) % 8
    c_all = jnp.concatenate([c_prompt, c_sample, jnp.zeros((pad, D_MODEL), F32)], axis=0)
    mod = _modulation(c_all, w_mod, b_mod)
    nl = mod.shape[0]
    mod_p = mod[:, :nb_p].reshape(nl, nb_p, 6, D_MODEL)
    mod_s = mod[:, nb_p:nb_p + nb_s].reshape(nl, nb_s, 6, D_MODEL)
    return _trunk(x_prompt, mod_p, p), _trunk(x_sample, mod_s, p)
```

```python
import functools
import math

import numpy as np
import jax
import jax.numpy as jnp
from jax import lax
from jax.experimental import pallas as pl
from jax.experimental.pallas import tpu as pltpu

D_MODEL = 1024
DEPTH = 4
GROUP_W = 256
N_SUBHEADS = 4
SUBHEAD_DIM = 64
IN_COLS = 6 * GROUP_W
CONV_WIDTH = 31
POOL_WINDOWS = (2, 4, 8, 16)
CHUNK = 128
N_EXPERTS = 32
TOP_K = 4
D_EXPERT = 512
SWIGLU_LIMIT = 7.0
SWIGLU_ALPHA = 1.702
DN_ALPHA = (2.0 * DEPTH) ** 0.25
LN_EPS = 1e-5

HALO = 16
FFT_ROWS = 8
FFT_COLS = 2048
SUBLANES = 8
BIG_ROWS = 128
VMEM_LIMIT = 48 * 1024 * 1024

F32 = jnp.float32
BF16 = jnp.bfloat16

_GU_ORDER = np.concatenate([np.arange(0, 2 * D_EXPERT, 2), np.arange(1, 2 * D_EXPERT, 2)])


def _tiles(seq):
    tile = min(512, seq)
    return tile, tile


def _dot(a, b):
    return jnp.dot(a, b, preferred_element_type=F32)


def _split(x):
    hi = x.astype(BF16)
    lo = (x - hi.astype(F32)).astype(BF16)
    return hi, lo


def _dot3(a, b_hi, b_lo):
    a_hi, a_lo = _split(a)
    return _dot(a_hi, b_hi) + _dot(a_lo, b_hi) + _dot(a_hi, b_lo)


def _ln(x):
    mu = jnp.mean(x, axis=-1, keepdims=True)
    d = x - mu
    var = jnp.mean(d * d, axis=-1, keepdims=True)
    return d * lax.rsqrt(var + LN_EPS)


def _params(*sem):
    return pltpu.CompilerParams(dimension_semantics=sem, vmem_limit_bytes=VMEM_LIMIT)


def _mod_kernel(c_ref, w_ref, b_ref, o_ref):
    c = c_ref[...]
    s = c * jax.nn.sigmoid(c)
    w_hi, w_lo = _split(w_ref[0])
    o_ref[0] = _dot3(s, w_hi, w_lo) + b_ref[0]


def _modulation(c_all, w_mod, b_mod):
    bp = c_all.shape[0]
    nl = w_mod.shape[0]
    ncol = w_mod.shape[2] // D_MODEL
    return pl.pallas_call(
        _mod_kernel,
        grid=(nl, ncol),
        in_specs=[pl.BlockSpec((bp, D_MODEL), lambda l, j: (0, 0)),
                  pl.BlockSpec((1, D_MODEL, D_MODEL), lambda l, j: (l, 0, j)),
                  pl.BlockSpec((1, 1, D_MODEL), lambda l, j: (l, 0, j))],
        out_specs=pl.BlockSpec((1, bp, D_MODEL), lambda l, j: (l, 0, j)),
        out_shape=jax.ShapeDtypeStruct((nl, bp, w_mod.shape[2]), F32),
        compiler_params=_params("arbitrary", "arbitrary"),
        name="modulation",
    )(c_all, w_mod, b_mod.reshape(nl, 1, -1))


def _fnet_prep_kernel(cc_ref, sc_ref, w_ref, m1_ref, m2_ref):
    w_hi, w_lo = _split(w_ref[0])
    m1_ref[0] = _dot3(cc_ref[...], w_hi, w_lo).astype(BF16)
    m2_ref[0] = _dot3(sc_ref[...], w_hi, w_lo).astype(BF16)


def _block_diag(w):
    nl, h, c, _ = w.shape
    eye = jnp.eye(h, dtype=w.dtype)
    return jnp.einsum("lhcd,hg->lhcgd", w, eye).reshape(nl, h * c, h * c)


def _fnet_prep(fnet_w):
    nl = fnet_w.shape[0]
    k = np.arange(SUBHEAD_DIM)
    ang = 2.0 * np.pi * np.outer(k, k) / SUBHEAD_DIM
    eye = np.eye(N_SUBHEADS)
    cc = jnp.asarray(np.kron(eye, np.cos(ang)), F32)
    sc = jnp.asarray(np.kron(eye, np.sin(ang)), F32)
    wbd = _block_diag(fnet_w)
    mat = pl.BlockSpec((GROUP_W, GROUP_W), lambda l: (0, 0))
    per_layer = pl.BlockSpec((1, GROUP_W, GROUP_W), lambda l: (l, 0, 0))
    return pl.pallas_call(
        _fnet_prep_kernel,
        grid=(nl,),
        in_specs=[mat, mat, per_layer],
        out_specs=[per_layer, per_layer],
        out_shape=[jax.ShapeDtypeStruct((nl, GROUP_W, GROUP_W), BF16)] * 2,
        compiler_params=_params("arbitrary"),
        name="fnet_prep",
    )(cc, sc, wbd)


def _fft_tables(seq):
    n1 = int(round(math.sqrt(seq)))
    n2 = seq // n1
    assert n1 * n2 == seq and n1 % FFT_ROWS == 0 and (n2 * GROUP_W) % FFT_COLS == 0
    scale = 1.0 / math.sqrt(seq * SUBHEAD_DIM)
    a = np.arange(n1)[:, None, None]
    k2 = np.arange(n2)[None, :, None]
    b = np.arange(n2)[None, None, :]
    ang = 2.0 * np.pi * (((a + n1 * b) * k2) % seq) / seq
    g_re = np.cos(ang) * scale
    g_im = -np.sin(ang) * scale
    k1 = np.arange(n1)
    ang2 = 2.0 * np.pi * (np.outer(k1, k1) % n1) / n1
    to = lambda t: jnp.asarray(t, F32).astype(BF16)
    return n1, n2, to(g_re), to(g_im), to(np.cos(ang2)), to(np.sin(ang2))


def _fft1_kernel(z_ref, gr_ref, gi_ref, vr_ref, vi_ref):
    for j in range(FFT_ROWS):
        xs = z_ref[0, :, j, :].astype(BF16)
        vr_ref[0, j] = _dot(gr_ref[j], xs).astype(BF16)
        vi_ref[0, j] = _dot(gi_ref[j], xs).astype(BF16)


def _fft2_kernel(vr_ref, vi_ref, c_ref, s_ref, m1_ref, m2_ref, o_ref):
    vr = vr_ref[0]
    vi = vi_ref[0]
    c = c_ref[...]
    s = s_ref[...]
    yr = _dot(c, vr) + _dot(s, vi)
    yi = _dot(c, vi) - _dot(s, vr)
    m1 = m1_ref[0]
    m2 = m2_ref[0]
    for g in range(FFT_COLS // GROUP_W):
        sl = slice(g * GROUP_W, (g + 1) * GROUP_W)
        o_ref[0, :, sl] = (_dot(yr[:, sl].astype(BF16), m1) + _dot(yi[:, sl].astype(BF16), m2)).astype(BF16)


def _fourier(z, batch, seq, tables, m1, m2, layer):
    n1, n2, g_re, g_im, c2, s2 = tables
    z4 = z.reshape(batch, n2, n1, IN_COLS)
    v_shape = jax.ShapeDtypeStruct((batch, n1, n2, GROUP_W), BF16)
    g_spec = pl.BlockSpec((FFT_ROWS, n2, n2), lambda b, i: (i, 0, 0))
    v_spec = pl.BlockSpec((1, FFT_ROWS, n2, GROUP_W), lambda b, i: (b, i, 0, 0))
    vr, vi = pl.pallas_call(
        _fft1_kernel,
        grid=(batch, n1 // FFT_ROWS),
        in_specs=[pl.BlockSpec((1, n2, FFT_ROWS, GROUP_W), lambda b, i: (b, 0, i, 0)), g_spec, g_spec],
        out_specs=[v_spec, v_spec],
        out_shape=[v_shape, v_shape],
        compiler_params=_params("arbitrary", "arbitrary"),
        name="fft_stage1",
    )(z4, g_re, g_im)
    width = n2 * GROUP_W
    vr = vr.reshape(batch, n1, width)
    vi = vi.reshape(batch, n1, width)
    blk = pl.BlockSpec((1, n1, FFT_COLS), lambda b, j: (b, 0, j))
    tab = pl.BlockSpec((n1, n1), lambda b, j: (0, 0))
    mat = pl.BlockSpec((1, GROUP_W, GROUP_W), lambda b, j: (layer, 0, 0))
    oa = pl.pallas_call(
        _fft2_kernel,
        grid=(batch, width // FFT_COLS),
        in_specs=[blk, blk, tab, tab, mat, mat],
        out_specs=blk,
        out_shape=jax.ShapeDtypeStruct((batch, n1, width), BF16),
        compiler_params=_params("arbitrary", "arbitrary"),
        name="fft_stage2",
    )(vr, vi, c2, s2, m1, m2)
    return oa.reshape(batch, seq, GROUP_W)


def _inproj_kernel(x_ref, mod_ref, w_ref, b_ref, z_ref):
    m = mod_ref[0]
    h = _ln(x_ref[0]) * (1.0 + m[1:2]) + m[0:1]
    z_ref[0] = _dot(h.astype(BF16), w_ref[0]) + b_ref[0]


def _inproj(x, mod, w_in, b_in, layer, tile):
    batch, seq, _ = x.shape
    return pl.pallas_call(
        _inproj_kernel,
        grid=(batch, seq // tile),
        in_specs=[pl.BlockSpec((1, tile, D_MODEL), lambda b, i: (b, i, 0)),
                  pl.BlockSpec((1, 6, D_MODEL), lambda b, i: (b, 0, 0)),
                  pl.BlockSpec((1, D_MODEL, IN_COLS), lambda b, i: (layer, 0, 0)),
                  pl.BlockSpec((1, 1, IN_COLS), lambda b, i: (layer, 0, 0))],
        out_specs=pl.BlockSpec((1, tile, IN_COLS), lambda b, i: (b, i, 0)),
        out_shape=jax.ShapeDtypeStruct((batch, seq, IN_COLS), F32),
        compiler_params=_params("arbitrary", "arbitrary"),
        name="in_proj",
    )(x, mod, w_in, b_in)


def _mix_kernel(seq, tile,
                x_ref, oa_ref,
                bvp_ref, bvm_ref, bvn_ref, bgp_ref, bgm_ref, bgn_ref,
                pcp_ref, pcm_ref, pcn_ref, du_ref, dv_ref, mod_ref,
                convw_ref, convb_ref, clng_ref, clnb_ref, cpw_ref,
                poolw_ref, pscale_ref,
                slng_ref, slnb_ref, wst_ref, sbias_ref, avg_ref,
                wout_ref, bout_ref, ln1g_ref, ln1b_ref,
                rwh_ref, rwl_ref, rb_ref, tri_ref, upper_ref,
                x1_ref, h2_ref, pos_ref, wts_ref, cnt_ref):
    i = pl.program_id(1)
    first = i == 0
    last = i == pl.num_programs(1) - 1
    m = mod_ref[0]

    def glu(v_ref, g_ref):
        return v_ref[0] * jax.nn.sigmoid(g_ref[0])

    full = tile + 2 * HALO
    gfull = jnp.concatenate([jnp.where(first, 0.0, glu(bvp_ref, bgp_ref)), glu(bvm_ref, bgm_ref),
                             jnp.where(last, 0.0, glu(bvn_ref, bgn_ref))], axis=0)
    acc = jnp.zeros((tile, GROUP_W), F32)
    off = HALO - CONV_WIDTH // 2
    for r in range(SUBLANES):
        shifted = pltpu.roll(gfull, full - (off + r), 0)
        for j in range(r, CONV_WIDTH, SUBLANES):
            acc = acc + shifted[j - r:j - r + tile] * convw_ref[0, j:j + 1, :]
    cv = _ln(acc + convb_ref[0]) * clng_ref[0] + clnb_ref[0]
    cv = cv * jax.nn.sigmoid(cv)
    ob = _dot(cv.astype(BF16), cpw_ref[0])

    pc = pcm_ref[0]
    win = jnp.concatenate([jnp.where(first, 0.0, pcp_ref[0]), pc, jnp.where(last, 0.0, pcn_ref[0])], axis=0)
    win = win + pltpu.roll(win, 1, 0)
    sums = [win]
    for step in (1, 2, 4):
        win = pltpu.roll(win, step, 0) + pltpu.roll(win, full - step, 0)
        sums.append(win)
    lane = lax.broadcasted_iota(jnp.int32, (tile, GROUP_W), 1)
    group = lane // SUBHEAD_DIM
    ssum = sums[0][HALO:HALO + tile]
    for gi in range(1, len(POOL_WINDOWS)):
        ssum = jnp.where(group == gi, sums[gi][HALO:HALO + tile], ssum)
    half = jnp.left_shift(1, group)
    pos = i * tile + lax.broadcasted_iota(jnp.int32, (tile, GROUP_W), 0)
    cnt = jnp.minimum(pos + half, seq) - jnp.maximum(pos - half, 0)
    pooled = ssum / cnt.astype(F32) - pc
    oc = _dot(pooled.astype(BF16), poolw_ref[0]) * pscale_ref[0]

    v = dv_ref[0]
    avg = avg_ref[...]
    v_hi, v_lo = _split(v)
    dv = v - (_dot(v_hi, avg) + _dot(v_lo, avg))
    q_hi, q_lo = _split(dv * dv)
    var = _dot(q_hi, avg) + _dot(q_lo, avg)
    vn = (dv * lax.rsqrt(var + LN_EPS) * slng_ref[0] + slnb_ref[0]).astype(BF16)
    wst = wst_ref[0]
    lane_c = lax.broadcasted_iota(jnp.int32, (CHUNK, GROUP_W), 1) // SUBHEAD_DIM
    mixed = []
    for c in range(tile // CHUNK):
        res = _dot(wst, vn[c * CHUNK:(c + 1) * CHUNK])
        sel = res[0:CHUNK]
        for h in range(1, N_SUBHEADS):
            sel = jnp.where(lane_c == h, res[h * CHUNK:(h + 1) * CHUNK], sel)
        mixed.append(sel + sbias_ref[0])
    od = du_ref[0] * jnp.concatenate(mixed, axis=0)

    g = GROUP_W
    mo = (_dot(oa_ref[0], wout_ref[0, 0:g]) + _dot(ob.astype(BF16), wout_ref[0, g:2 * g])
          + _dot(oc.astype(BF16), wout_ref[0, 2 * g:3 * g]) + _dot(od.astype(BF16), wout_ref[0, 3 * g:4 * g])
          + bout_ref[0])
    x1 = _ln(DN_ALPHA * x_ref[0] + m[2:3] * mo) * ln1g_ref[0] + ln1b_ref[0]
    x1_ref[0] = x1

    h2 = _ln(x1) * (1.0 + m[4:5]) + m[3:4]
    h2_ref[0] = h2
    logits = _dot3(h2, rwh_ref[0], rwl_ref[0]) + rb_ref[0]
    eio = lax.broadcasted_iota(jnp.int32, (tile, N_EXPERTS), 1)
    kio = lax.broadcasted_iota(jnp.int32, (tile, TOP_K), 1)
    work = logits
    vals, hots = [], []
    for _ in range(TOP_K):
        mx = jnp.max(work, axis=1, keepdims=True)
        am = jnp.min(jnp.where(work == mx, eio, N_EXPERTS), axis=1, keepdims=True)
        hot = eio == am
        vals.append(mx)
        hots.append(hot)
        work = jnp.where(hot, -jnp.inf, work)
    exps = [jnp.exp(vv - vals[0]) for vv in vals]
    denom = exps[0] + exps[1] + exps[2] + exps[3]
    onehot = jnp.zeros((tile, N_EXPERTS), F32)
    for hot in hots:
        onehot = onehot + hot.astype(F32)
    before = _dot(tri_ref[...], onehot.astype(BF16))
    cnt = jnp.sum(onehot, axis=0, keepdims=True).astype(jnp.int32)
    cnt8 = jnp.broadcast_to(jnp.right_shift(cnt + (SUBLANES - 1), 3), (SUBLANES, N_EXPERTS))
    seg = _dot(cnt8.astype(BF16), upper_ref[...])[0:1] * float(SUBLANES)
    where_to = seg + before
    pos = jnp.zeros((tile, TOP_K), jnp.int32)
    wts = jnp.zeros((tile, TOP_K), F32)
    for k in range(TOP_K):
        pk = jnp.sum(jnp.where(hots[k], where_to, 0.0), axis=1, keepdims=True).astype(jnp.int32)
        pos = jnp.where(kio == k, pk, pos)
        wts = jnp.where(kio == k, exps[k] / denom, wts)
    pos_ref[...] = pos
    wts_ref[...] = wts
    cnt_ref[0] = cnt


def _mix(x, z, oa, mod, p, layer, tile):
    batch, seq, _ = x.shape
    nt = seq // tile
    hb = tile // HALO
    nhb = seq // HALO

    def col(c):
        return pl.BlockSpec((1, tile, GROUP_W), lambda b, i: (b, i, c))

    def prev(c):
        return pl.BlockSpec((1, HALO, GROUP_W), lambda b, i: (b, jnp.maximum(i * hb - 1, 0), c))

    def nxt(c):
        return pl.BlockSpec((1, HALO, GROUP_W), lambda b, i: (b, jnp.minimum((i + 1) * hb, nhb - 1), c))

    def lay(*shape):
        return pl.BlockSpec((1,) + shape, lambda b, i: (layer,) + (0,) * len(shape))

    def const(*shape):
        return pl.BlockSpec(shape, lambda b, i: (0,) * len(shape))

    row = pl.BlockSpec((1, tile, D_MODEL), lambda b, i: (b, i, 0))
    sel = pl.BlockSpec((tile, TOP_K), lambda b, i: (b * nt + i, 0))
    in_specs = [
        row, col(0),
        prev(1), col(1), nxt(1), prev(2), col(2), nxt(2),
        prev(3), col(3), nxt(3), col(4), col(5),
        pl.BlockSpec((1, 6, D_MODEL), lambda b, i: (b, 0, 0)),
        lay(CONV_WIDTH, GROUP_W), lay(1, GROUP_W), lay(1, GROUP_W), lay(1, GROUP_W), lay(GROUP_W, GROUP_W),
        lay(GROUP_W, GROUP_W), lay(1, GROUP_W),
        lay(1, GROUP_W), lay(1, GROUP_W), lay(N_SUBHEADS * CHUNK, CHUNK), lay(CHUNK, GROUP_W),
        const(GROUP_W, GROUP_W),
        lay(D_MODEL, D_MODEL), lay(1, D_MODEL), lay(1, D_MODEL), lay(1, D_MODEL),
        lay(D_MODEL, N_EXPERTS), lay(D_MODEL, N_EXPERTS), lay(1, N_EXPERTS),
        const(tile, tile), const(N_EXPERTS, N_EXPERTS),
    ]
    out_specs = [row, row, sel, sel,
                 pl.BlockSpec((1, 1, N_EXPERTS), lambda b, i: (b * nt + i, 0, 0))]
    out_shape = [jax.ShapeDtypeStruct((batch, seq, D_MODEL), F32),
                 jax.ShapeDtypeStruct((batch, seq, D_MODEL), F32),
                 jax.ShapeDtypeStruct((batch * seq, TOP_K), jnp.int32),
                 jax.ShapeDtypeStruct((batch * seq, TOP_K), F32),
                 jax.ShapeDtypeStruct((batch * nt, 1, N_EXPERTS), jnp.int32)]
    tri = jnp.asarray(np.tril(np.ones((tile, tile), np.float32), -1), BF16)
    upper = jnp.asarray(np.triu(np.ones((N_EXPERTS, N_EXPERTS), np.float32), 1), BF16)
    return pl.pallas_call(
        functools.partial(_mix_kernel, seq, tile),
        grid=(batch, nt),
        in_specs=in_specs,
        out_specs=out_specs,
        out_shape=out_shape,
        compiler_params=_params("arbitrary", "arbitrary"),
        name="mix",
    )(x, oa, z, z, z, z, z, z, z, z, z, z, z, mod,
      p["conv_w"], p["conv_b"], p["conv_ln_g"], p["conv_ln_b"], p["conv_pw"],
      p["pool_w"], p["pool_scale"],
      p["sgu_ln_g"], p["sgu_ln_b"], p["sgu_w"], p["sgu_b"], p["avg"],
      p["w_out"], p["b_out"], p["ln1_g"], p["ln1_b"],
      p["router_hi"], p["router_lo"], p["router_b"], tri, upper)


def _sorted_rows(tile):
    return TOP_K * tile + N_EXPERTS * SUBLANES


def _route(cnt, tile, rows):
    cnt = cnt.reshape(-1, N_EXPERTS)
    n_tiles = cnt.shape[0]
    r8 = rows // SUBLANES
    cnt8 = (cnt + SUBLANES - 1) // SUBLANES
    seg8 = jnp.cumsum(cnt8, axis=1) - cnt8
    tot8 = jnp.sum(cnt8, axis=0)
    pad8 = ((tot8 + r8 - 1) // r8) * r8
    cum8 = jnp.cumsum(pad8)
    start8 = cum8 - pad8
    gb8 = start8[None, :] + jnp.cumsum(cnt8, axis=0) - cnt8
    n_blocks = -(-(n_tiles * tile * TOP_K + (SUBLANES - 1) * N_EXPERTS * n_tiles) // rows) + N_EXPERTS
    block_start8 = jnp.arange(n_blocks, dtype=cum8.dtype) * r8
    block_e = jnp.minimum(jnp.sum(cum8[None, :] <= block_start8[:, None], axis=1), N_EXPERTS - 1)
    used = (cum8[-1:] // r8)
    tile_units = jnp.broadcast_to(jnp.sum(cnt8, axis=1, keepdims=True), cnt8.shape)
    segs = jnp.stack([seg8, cnt8, gb8, tile_units], axis=1).reshape(-1)
    tails = jnp.concatenate([start8 + tot8, pad8 - tot8, used, jnp.zeros((2 * N_EXPERTS - 1,), tot8.dtype)])
    i32 = lambda t: t.astype(jnp.int32)
    return i32(segs), i32(tails), i32(block_e), i32(used), n_blocks


def _segment_chunks(units, bits, visit):
    for b in range(bits):
        @pl.when(((units >> b) & 1) == 1)
        def _():
            visit(((units >> (b + 1)) << (b + 1)) * SUBLANES, SUBLANES << b)


def _start_segment_copies(seg_ref, tile_buf, sorted_hbm, sem, to_hbm):
    low_bits = (BIG_ROWS // SUBLANES).bit_length() - 1

    def body(e, carry):
        seg = seg_ref[e] * SUBLANES
        glob = seg_ref[2 * N_EXPERTS + e] * SUBLANES
        units = seg_ref[N_EXPERTS + e]

        def start(off, size):
            a = tile_buf.at[pl.ds(pl.multiple_of(seg + off, SUBLANES), size)]
            b = sorted_hbm.at[pl.ds(pl.multiple_of(glob + off, SUBLANES), size)]
            (pltpu.make_async_copy(a, b, sem) if to_hbm else pltpu.make_async_copy(b, a, sem)).start()

        n_big = units >> low_bits

        def big(c, carry):
            start(c * BIG_ROWS, BIG_ROWS)
            return carry

        lax.fori_loop(0, n_big, big, 0)
        _segment_chunks(units & ((1 << low_bits) - 1), low_bits, lambda off, size: start(n_big * BIG_ROWS + off, size))
        return carry

    lax.fori_loop(0, N_EXPERTS, body, 0)


def _wait_rows(units, bits, hbm, sem):
    def visit(off, size):
        del off
        pltpu.make_async_copy(hbm.at[pl.ds(0, size)], hbm.at[pl.ds(0, size)], sem).wait()

    _segment_chunks(units, bits, visit)


def _sorted_chunks(q):
    n = next(n for n in (3, 2, 1) if (q // 128) % n == 0)
    return [(c * (q // n), q // n) for c in range(n)]


def _dispatch_kernel(tile, seg_bits, tail_bits, seg_ref, prev_ref, tail_ref, pos_ref, h2_ref, xs_hbm,
                     xbuf, zbuf, sems):
    i = pl.program_id(0)
    slot = i % 2
    q = xbuf.shape[1]
    pos = pos_ref[...]
    lane = lax.broadcasted_iota(jnp.int32, (tile, 128), 1)
    canvas = jnp.full((tile, 128), -1.0, F32)
    for k in range(TOP_K):
        canvas = jnp.where(lane == k, pos[:, k:k + 1].astype(F32), canvas)
    pos_t = canvas.T.astype(jnp.int32)
    h2b = h2_ref[...].astype(BF16)
    for start, size in _sorted_chunks(q):
        qi = start + lax.broadcasted_iota(jnp.int32, (size, tile), 0)
        hit = qi == pos_t[0:1]
        for k in range(1, TOP_K):
            hit = hit | (qi == pos_t[k:k + 1])
        xbuf[slot, start:start + size] = _dot(jnp.where(hit, 1.0, 0.0).astype(BF16), h2b)

    @pl.when(i > 0)
    def _():
        _wait_rows(prev_ref[3 * N_EXPERTS], seg_bits, xs_hbm, sems.at[1 - slot])

    _start_segment_copies(seg_ref, xbuf.at[slot], xs_hbm, sems.at[slot], True)

    @pl.when(i == 0)
    def _():
        zbuf[...] = jnp.zeros(zbuf.shape, F32)

    @pl.when(i == pl.num_programs(0) - 1)
    def _():
        _wait_rows(seg_ref[3 * N_EXPERTS], seg_bits, xs_hbm, sems.at[slot])
        sem = sems.at[0]

        def tails(action):
            def body(e, carry):
                def visit(off, size):
                    dst = xs_hbm.at[pl.ds(pl.multiple_of(tail_ref[e] * SUBLANES + off, SUBLANES), size)]
                    action(pltpu.make_async_copy(zbuf.at[pl.ds(0, size)], dst, sem))

                _segment_chunks(tail_ref[N_EXPERTS + e], tail_bits, visit)
                return carry

            lax.fori_loop(0, N_EXPERTS, body, 0)

        tails(lambda cp: cp.start())
        tails(lambda cp: cp.wait())

        rows = zbuf.shape[0]

        def fill(g, carry):
            cp = pltpu.make_async_copy(zbuf, xs_hbm.at[pl.ds(pl.multiple_of(g * rows, rows), rows)], sem)
            cp.start()
            cp.wait()
            return carry

        lax.fori_loop(tail_ref[2 * N_EXPERTS], xs_hbm.shape[0] // rows, fill, 0)


def _dispatch(segs, tails, pos, h2, total_rows, tile, rows):
    n = h2.shape[0]
    seg_bits = (_sorted_rows(tile) // SUBLANES).bit_length()
    tail_bits = (rows // SUBLANES - 1).bit_length()
    return pl.pallas_call(
        functools.partial(_dispatch_kernel, tile, seg_bits, tail_bits),
        grid=(n // tile,),
        in_specs=[pl.BlockSpec((4 * N_EXPERTS,), lambda i: (i,), memory_space=pltpu.SMEM),
                  pl.BlockSpec((4 * N_EXPERTS,), lambda i: (jnp.maximum(i - 1, 0),), memory_space=pltpu.SMEM),
                  pl.BlockSpec((4 * N_EXPERTS,), lambda i: (0,), memory_space=pltpu.SMEM),
                  pl.BlockSpec((tile, TOP_K), lambda i: (i, 0)),
                  pl.BlockSpec((tile, D_MODEL), lambda i: (i, 0))],
        out_specs=pl.BlockSpec(memory_space=pl.ANY),
        out_shape=jax.ShapeDtypeStruct((total_rows, D_MODEL), F32),
        scratch_shapes=[pltpu.VMEM((2, _sorted_rows(tile), D_MODEL), F32),
                        pltpu.VMEM((rows, D_MODEL), F32),
                        pltpu.SemaphoreType.DMA((2,))],
        compiler_params=_params("arbitrary"),
        name="dispatch",
    )(segs, segs, tails, pos, h2)


def _gu_prep_kernel(w_ref, perm_ref, o_ref):
    o_ref[0, 0] = _dot(w_ref[0, 0].astype(BF16), perm_ref[...]).astype(BF16)


def _gu_prep(w_gu):
    nl, ne, d, c = w_gu.shape
    perm = np.zeros((c, c), np.float32)
    perm[_GU_ORDER, np.arange(c)] = 1.0
    blk = pl.BlockSpec((1, 1, d, c), lambda l, e: (l, e, 0, 0))
    return pl.pallas_call(
        _gu_prep_kernel,
        grid=(nl, ne),
        in_specs=[blk, pl.BlockSpec((c, c), lambda l, e: (0, 0))],
        out_specs=blk,
        out_shape=jax.ShapeDtypeStruct(w_gu.shape, BF16),
        compiler_params=_params("arbitrary", "arbitrary"),
        name="gu_prep",
    )(w_gu, jnp.asarray(perm, BF16))


def _expert_kernel(be_ref, used_ref, xs_ref, wgu_ref, bgu_ref, wd_ref, bd_ref, y_ref):
    del be_ref

    @pl.when(pl.program_id(0) >= used_ref[0])
    def _():
        y_ref[...] = jnp.zeros(y_ref.shape, F32)

    @pl.when(pl.program_id(0) < used_ref[0])
    def _():
        hu = _dot(xs_ref[...].astype(BF16), wgu_ref[0, 0]) + bgu_ref[0, 0]
        glu = jnp.minimum(hu[:, :D_EXPERT], SWIGLU_LIMIT)
        lin = jnp.clip(hu[:, D_EXPERT:], -SWIGLU_LIMIT, SWIGLU_LIMIT)
        act = glu * jax.nn.sigmoid(SWIGLU_ALPHA * glu) * (lin + 1.0)
        y_ref[...] = _dot(act.astype(BF16), wd_ref[0, 0]) + bd_ref[0, 0]


def _experts(block_e, used, xs, p, layer, rows, n_blocks):
    def wspec(*shape):
        return pl.BlockSpec((1, 1) + shape, lambda g, be, u: (layer, be[g]) + (0,) * len(shape))

    grid_spec = pltpu.PrefetchScalarGridSpec(
        num_scalar_prefetch=2,
        grid=(n_blocks,),
        in_specs=[pl.BlockSpec((rows, D_MODEL), lambda g, be, u: (g, 0)),
                  wspec(D_MODEL, 2 * D_EXPERT), wspec(1, 2 * D_EXPERT),
                  wspec(D_EXPERT, D_MODEL), wspec(1, D_MODEL)],
        out_specs=pl.BlockSpec((rows, D_MODEL), lambda g, be, u: (g, 0)),
    )
    return pl.pallas_call(
        _expert_kernel,
        grid_spec=grid_spec,
        out_shape=jax.ShapeDtypeStruct(xs.shape, F32),
        compiler_params=_params("arbitrary"),
        name="experts",
    )(block_e, used, xs, p["w_gu"], p["b_gu"], p["w_dn"], p["b_dn"])


def _combine_kernel(tile, seg_bits, seg_ref, next_ref, y_hbm, pos_ref, x1_ref, wts_ref, mod_ref, g_ref, b_ref,
                    o_ref, ybuf, sems):
    t = pl.program_id(0) * pl.num_programs(1) + pl.program_id(1)
    n_tiles = pl.num_programs(0) * pl.num_programs(1)
    slot = t % 2

    @pl.when(t == 0)
    def _():
        ybuf[...] = jnp.zeros(ybuf.shape, F32)
        _start_segment_copies(seg_ref, ybuf.at[0], y_hbm, sems.at[0], False)

    @pl.when(t + 1 < n_tiles)
    def _():
        _start_segment_copies(next_ref, ybuf.at[1 - slot], y_hbm, sems.at[1 - slot], False)

    _wait_rows(seg_ref[3 * N_EXPERTS], seg_bits, y_hbm, sems.at[slot])
    pos = pos_ref[...]
    wts = wts_ref[...]
    f = jnp.zeros((tile, D_MODEL), F32)
    for start, size in _sorted_chunks(ybuf.shape[1]):
        qi = start + lax.broadcasted_iota(jnp.int32, (tile, size), 1)
        wm = jnp.zeros((tile, size), F32)
        for k in range(TOP_K):
            wm = jnp.where(qi == pos[:, k:k + 1], wts[:, k:k + 1], wm)
        f = f + _dot(wm.astype(BF16), ybuf[slot, start:start + size].astype(BF16))
    m = mod_ref[0]
    o_ref[0] = _ln(DN_ALPHA * x1_ref[0] + m[5:6] * f) * g_ref[0] + b_ref[0]


def _combine(segs, pos, y, x1, wts, mod, p, layer, tile):
    batch, seq, _ = x1.shape
    nt = seq // tile
    last = batch * nt - 1
    row = pl.BlockSpec((1, tile, D_MODEL), lambda b, i: (b, i, 0))
    sel = pl.BlockSpec((tile, TOP_K), lambda b, i: (b * nt + i, 0))
    vec = pl.BlockSpec((1, 1, D_MODEL), lambda b, i: (layer, 0, 0))
    return pl.pallas_call(
        functools.partial(_combine_kernel, tile, (_sorted_rows(tile) // SUBLANES).bit_length()),
        grid=(batch, nt),
        in_specs=[pl.BlockSpec((4 * N_EXPERTS,), lambda b, i: (b * nt + i,), memory_space=pltpu.SMEM),
                  pl.BlockSpec((4 * N_EXPERTS,), lambda b, i: (jnp.minimum(b * nt + i + 1, last),),
                               memory_space=pltpu.SMEM),
                  pl.BlockSpec(memory_space=pl.ANY),
                  sel, row, sel,
                  pl.BlockSpec((1, 6, D_MODEL), lambda b, i: (b, 0, 0)),
                  vec, vec],
        out_specs=row,
        out_shape=jax.ShapeDtypeStruct(x1.shape, F32),
        scratch_shapes=[pltpu.VMEM((2, _sorted_rows(tile), D_MODEL), F32), pltpu.SemaphoreType.DMA((2,))],
        compiler_params=_params("arbitrary", "arbitrary"),
        name="combine",
    )(segs, segs, y, pos, x1, wts, mod, p["ln2_g"], p["ln2_b"])


def _prepare(w_in, b_in, fnet_w, conv_w, conv_b, conv_ln_g, conv_ln_b, conv_pw, pool_w, pool_scale,
             sgu_ln_g, sgu_ln_b, sgu_w, sgu_b, w_out, b_out, ln1_g, ln1_b, router_w, router_b,
             w_gu, b_gu, w_dn, b_dn, ln2_g, ln2_b):
    nl = w_in.shape[0]
    vec = lambda t: t.reshape(nl, 1, -1)
    group = np.arange(GROUP_W) // SUBHEAD_DIM
    avg = (group[:, None] == group[None, :]).astype(np.float32) / SUBHEAD_DIM
    r_hi = router_w.astype(BF16)
    m1, m2 = _fnet_prep(fnet_w)
    return {
        "w_in": w_in.astype(BF16), "b_in": vec(b_in), "fnet_m1": m1, "fnet_m2": m2,
        "conv_w": conv_w, "conv_b": vec(conv_b), "conv_ln_g": vec(conv_ln_g), "conv_ln_b": vec(conv_ln_b),
        "conv_pw": conv_pw.astype(BF16),
        "pool_w": _block_diag(pool_w).astype(BF16),
        "pool_scale": vec(pool_scale),
        "sgu_ln_g": vec(sgu_ln_g), "sgu_ln_b": vec(sgu_ln_b),
        "sgu_w": sgu_w.reshape(nl, N_SUBHEADS * CHUNK, CHUNK).astype(BF16),
        "sgu_b": jnp.repeat(jnp.swapaxes(sgu_b, 1, 2), SUBHEAD_DIM, axis=2),
        "avg": jnp.asarray(avg, BF16),
        "w_out": w_out.astype(BF16), "b_out": vec(b_out), "ln1_g": vec(ln1_g), "ln1_b": vec(ln1_b),
        "router_hi": r_hi, "router_lo": (router_w - r_hi.astype(F32)).astype(BF16), "router_b": vec(router_b),
        "w_gu": _gu_prep(w_gu), "b_gu": b_gu[..., None, _GU_ORDER],
        "w_dn": w_dn.astype(BF16), "b_dn": b_dn[..., None, :],
        "ln2_g": vec(ln2_g), "ln2_b": vec(ln2_b),
    }


def _trunk(x, mod, p):
    batch, seq, _ = x.shape
    tile, rows = _tiles(seq)
    tables = _fft_tables(seq)
    for layer in range(DEPTH):
        z = _inproj(x, mod[layer], p["w_in"], p["b_in"], layer, tile)
        oa = _fourier(z, batch, seq, tables, p["fnet_m1"], p["fnet_m2"], layer)
        x1, h2, pos, wts, cnt = _mix(x, z, oa, mod[layer], p, layer, tile)
        segs, tails, block_e, used, n_blocks = _route(cnt, tile, rows)
        xs = _dispatch(segs, tails, pos, h2.reshape(batch * seq, D_MODEL), n_blocks * rows, tile, rows)
        y = _experts(block_e, used, xs, p, layer, rows, n_blocks)
        x = _combine(segs, pos, y, x1, wts, mod[layer], p, layer, tile)
    return x


def kernel(x_prompt, x_sample, c_prompt, c_sample, w_mod, b_mod, w_in, b_in, fnet_w, conv_w, conv_b,
           conv_ln_g, conv_ln_b, conv_pw, pool_w, pool_scale, sgu_ln_g, sgu_ln_b, sgu_w, sgu_b,
           w_out, b_out, ln1_g, ln1_b, router_w, router_b, w_gu, b_gu, w_dn, b_dn, ln2_g, ln2_b):
    p = _prepare(w_in, b_in, fnet_w, conv_w, conv_b, conv_ln_g, conv_ln_b, conv_pw, pool_w, pool_scale,
                 sgu_ln_g, sgu_ln_b, sgu_w, sgu_b, w_out, b_out, ln1_g, ln1_b, router_w, router_b,
                 w_gu, b_gu, w_dn, b_dn, ln2_g, ln2_b)
    nb_p, nb_s = c_prompt.shape[0], c_sample.shape[0]
    pad = (-(nb_p + nb_s)) % 8
    c_all = jnp.concatenate([c_prompt, c_sample, jnp.zeros((pad, D_MODEL), F32)], axis=0)
    mod = _modulation(c_all, w_mod, b_mod)
    nl = mod.shape[0]
    mod_p = mod[:, :nb_p].reshape(nl, nb_p, 6, D_MODEL)
    mod_s = mod[:, nb_p:nb_p + nb_s].reshape(nl, nb_s, 6, D_MODEL)
    return _trunk(x_prompt, mod_p, p), _trunk(x_sample, mod_s, p)
```

```python
import functools
import math

import numpy as np
import jax
import jax.numpy as jnp
from jax import lax
from jax.experimental import pallas as pl
from jax.experimental.pallas import tpu as pltpu

D_MODEL = 1024
DEPTH = 4
GROUP_W = 256
N_SUBHEADS = 4
SUBHEAD_DIM = 64
IN_COLS = 6 * GROUP_W
CONV_WIDTH = 31
POOL_WINDOWS = (2, 4, 8, 16)
CHUNK = 128
N_EXPERTS = 32
TOP_K = 4
D_EXPERT = 512
SWIGLU_LIMIT = 7.0
SWIGLU_ALPHA = 1.702
DN_ALPHA = (2.0 * DEPTH) ** 0.25
LN_EPS = 1e-5

HALO = 16
FFT_ROWS = 16
FFT_COLS = 2048
SUBLANES = 8
BIG_ROWS = 128
VMEM_LIMIT = 48 * 1024 * 1024

F32 = jnp.float32
BF16 = jnp.bfloat16

_GU_ORDER = np.concatenate([np.arange(0, 2 * D_EXPERT, 2), np.arange(1, 2 * D_EXPERT, 2)])


def _tiles(seq):
    tile = min(512, seq)
    return tile, tile


def _dot(a, b):
    return jnp.dot(a, b, preferred_element_type=F32)


def _split(x):
    hi = x.astype(BF16)
    lo = (x - hi.astype(F32)).astype(BF16)
    return hi, lo


def _dot3(a, b_hi, b_lo):
    a_hi, a_lo = _split(a)
    return _dot(a_hi, b_hi) + _dot(a_lo, b_hi) + _dot(a_hi, b_lo)


def _ln(x):
    mu = jnp.mean(x, axis=-1, keepdims=True)
    d = x - mu
    var = jnp.mean(d * d, axis=-1, keepdims=True)
    return d * lax.rsqrt(var + LN_EPS)


def _params(*sem):
    return pltpu.CompilerParams(dimension_semantics=sem, vmem_limit_bytes=VMEM_LIMIT)


def _mod_kernel(c_ref, w_ref, b_ref, o_ref):
    c = c_ref[...]
    s = c * jax.nn.sigmoid(c)
    w_hi, w_lo = _split(w_ref[0])
    o_ref[0] = _dot3(s, w_hi, w_lo) + b_ref[0]


def _modulation(c_all, w_mod, b_mod):
    bp = c_all.shape[0]
    nl = w_mod.shape[0]
    ncol = w_mod.shape[2] // D_MODEL
    return pl.pallas_call(
        _mod_kernel,
        grid=(nl, ncol),
        in_specs=[pl.BlockSpec((bp, D_MODEL), lambda l, j: (0, 0)),
                  pl.BlockSpec((1, D_MODEL, D_MODEL), lambda l, j: (l, 0, j)),
                  pl.BlockSpec((1, 1, D_MODEL), lambda l, j: (l, 0, j))],
        out_specs=pl.BlockSpec((1, bp, D_MODEL), lambda l, j: (l, 0, j)),
        out_shape=jax.ShapeDtypeStruct((nl, bp, w_mod.shape[2]), F32),
        compiler_params=_params("arbitrary", "arbitrary"),
        name="modulation",
    )(c_all, w_mod, b_mod.reshape(nl, 1, -1))


def _fnet_prep_kernel(cc_ref, sc_ref, w_ref, m1_ref, m2_ref):
    w_hi, w_lo = _split(w_ref[0])
    m1_ref[0] = _dot3(cc_ref[...], w_hi, w_lo).astype(BF16)
    m2_ref[0] = _dot3(sc_ref[...], w_hi, w_lo).astype(BF16)


def _block_diag(w):
    nl, h, c, _ = w.shape
    eye = jnp.eye(h, dtype=w.dtype)
    return jnp.einsum("lhcd,hg->lhcgd", w, eye).reshape(nl, h * c, h * c)


def _fnet_prep(fnet_w):
    nl = fnet_w.shape[0]
    k = np.arange(SUBHEAD_DIM)
    ang = 2.0 * np.pi * np.outer(k, k) / SUBHEAD_DIM
    eye = np.eye(N_SUBHEADS)
    cc = jnp.asarray(np.kron(eye, np.cos(ang)), F32)
    sc = jnp.asarray(np.kron(eye, np.sin(ang)), F32)
    wbd = _block_diag(fnet_w)
    mat = pl.BlockSpec((GROUP_W, GROUP_W), lambda l: (0, 0))
    per_layer = pl.BlockSpec((1, GROUP_W, GROUP_W), lambda l: (l, 0, 0))
    return pl.pallas_call(
        _fnet_prep_kernel,
        grid=(nl,),
        in_specs=[mat, mat, per_layer],
        out_specs=[per_layer, per_layer],
        out_shape=[jax.ShapeDtypeStruct((nl, GROUP_W, GROUP_W), BF16)] * 2,
        compiler_params=_params("arbitrary"),
        name="fnet_prep",
    )(cc, sc, wbd)


def _fft_tables(seq):
    n1 = int(round(math.sqrt(seq)))
    n2 = seq // n1
    assert n1 * n2 == seq and n1 % FFT_ROWS == 0 and (n2 * GROUP_W) % FFT_COLS == 0
    scale = 1.0 / math.sqrt(seq * SUBHEAD_DIM)
    a = np.arange(n1)[:, None, None]
    k2 = np.arange(n2)[None, :, None]
    b = np.arange(n2)[None, None, :]
    ang = 2.0 * np.pi * (((a + n1 * b) * k2) % seq) / seq
    g_re = np.cos(ang) * scale
    g_im = -np.sin(ang) * scale
    k1 = np.arange(n1)
    ang2 = 2.0 * np.pi * (np.outer(k1, k1) % n1) / n1
    to = lambda t: jnp.asarray(t, F32).astype(BF16)
    return n1, n2, to(g_re), to(g_im), to(np.cos(ang2)), to(np.sin(ang2))


def _fft1_kernel(z_ref, gr_ref, gi_ref, vr_ref, vi_ref):
    for j in range(FFT_ROWS):
        xs = z_ref[0, :, j, :].astype(BF16)
        vr_ref[0, j] = _dot(gr_ref[j], xs).astype(BF16)
        vi_ref[0, j] = _dot(gi_ref[j], xs).astype(BF16)


def _fft2_kernel(vr_ref, vi_ref, c_ref, s_ref, m1_ref, m2_ref, o_ref):
    vr = vr_ref[0]
    vi = vi_ref[0]
    c = c_ref[...]
    s = s_ref[...]
    yr = _dot(c, vr) + _dot(s, vi)
    yi = _dot(c, vi) - _dot(s, vr)
    m1 = m1_ref[0]
    m2 = m2_ref[0]
    for g in range(FFT_COLS // GROUP_W):
        sl = slice(g * GROUP_W, (g + 1) * GROUP_W)
        o_ref[0, :, sl] = (_dot(yr[:, sl].astype(BF16), m1) + _dot(yi[:, sl].astype(BF16), m2)).astype(BF16)


def _fourier(z, batch, seq, tables, m1, m2, layer):
    n1, n2, g_re, g_im, c2, s2 = tables
    z4 = z.reshape(batch, n2, n1, IN_COLS)
    v_shape = jax.ShapeDtypeStruct((batch, n1, n2, GROUP_W), BF16)
    g_spec = pl.BlockSpec((FFT_ROWS, n2, n2), lambda b, i: (i, 0, 0))
    v_spec = pl.BlockSpec((1, FFT_ROWS, n2, GROUP_W), lambda b, i: (b, i, 0, 0))
    vr, vi = pl.pallas_call(
        _fft1_kernel,
        grid=(batch, n1 // FFT_ROWS),
        in_specs=[pl.BlockSpec((1, n2, FFT_ROWS, GROUP_W), lambda b, i: (b, 0, i, 0)), g_spec, g_spec],
        out_specs=[v_spec, v_spec],
        out_shape=[v_shape, v_shape],
        compiler_params=_params("arbitrary", "arbitrary"),
        name="fft_stage1",
    )(z4, g_re, g_im)
    width = n2 * GROUP_W
    vr = vr.reshape(batch, n1, width)
    vi = vi.reshape(batch, n1, width)
    blk = pl.BlockSpec((1, n1, FFT_COLS), lambda b, j: (b, 0, j))
    tab = pl.BlockSpec((n1, n1), lambda b, j: (0, 0))
    mat = pl.BlockSpec((1, GROUP_W, GROUP_W), lambda b, j: (layer, 0, 0))
    oa = pl.pallas_call(
        _fft2_kernel,
        grid=(batch, width // FFT_COLS),
        in_specs=[blk, blk, tab, tab, mat, mat],
        out_specs=blk,
        out_shape=jax.ShapeDtypeStruct((batch, n1, width), BF16),
        compiler_params=_params("arbitrary", "arbitrary"),
        name="fft_stage2",
    )(vr, vi, c2, s2, m1, m2)
    return oa.reshape(batch, seq, GROUP_W)


def _inproj_kernel(x_ref, mod_ref, w_ref, b_ref, z_ref):
    m = mod_ref[0]
    h = _ln(x_ref[0]) * (1.0 + m[1:2]) + m[0:1]
    z_ref[0] = _dot(h.astype(BF16), w_ref[0]) + b_ref[0]


def _inproj(x, mod, w_in, b_in, layer, tile):
    batch, seq, _ = x.shape
    return pl.pallas_call(
        _inproj_kernel,
        grid=(batch, seq // tile),
        in_specs=[pl.BlockSpec((1, tile, D_MODEL), lambda b, i: (b, i, 0)),
                  pl.BlockSpec((1, 6, D_MODEL), lambda b, i: (b, 0, 0)),
                  pl.BlockSpec((1, D_MODEL, IN_COLS), lambda b, i: (layer, 0, 0)),
                  pl.BlockSpec((1, 1, IN_COLS), lambda b, i: (layer, 0, 0))],
        out_specs=pl.BlockSpec((1, tile, IN_COLS), lambda b, i: (b, i, 0)),
        out_shape=jax.ShapeDtypeStruct((batch, seq, IN_COLS), F32),
        compiler_params=_params("arbitrary", "arbitrary"),
        name="in_proj",
    )(x, mod, w_in, b_in)


def _mix_kernel(seq, tile,
                x_ref, oa_ref,
                bvp_ref, bvm_ref, bvn_ref, bgp_ref, bgm_ref, bgn_ref,
                pcp_ref, pcm_ref, pcn_ref, du_ref, dv_ref, mod_ref,
                convw_ref, convb_ref, clng_ref, clnb_ref, cpw_ref,
                poolw_ref, pscale_ref,
                slng_ref, slnb_ref, wst_ref, sbias_ref, avg_ref,
                wout_ref, bout_ref, ln1g_ref, ln1b_ref,
                rwh_ref, rwl_ref, rb_ref, tri_ref, upper_ref,
                x1_ref, h2_ref, pos_ref, wts_ref, cnt_ref):
    i = pl.program_id(1)
    first = i == 0
    last = i == pl.num_programs(1) - 1
    m = mod_ref[0]

    def glu(v_ref, g_ref):
        return v_ref[0] * jax.nn.sigmoid(g_ref[0])

    full = tile + 2 * HALO
    gfull = jnp.concatenate([jnp.where(first, 0.0, glu(bvp_ref, bgp_ref)), glu(bvm_ref, bgm_ref),
                             jnp.where(last, 0.0, glu(bvn_ref, bgn_ref))], axis=0)
    acc = jnp.zeros((tile, GROUP_W), F32)
    off = HALO - CONV_WIDTH // 2
    for r in range(SUBLANES):
        shifted = pltpu.roll(gfull, full - (off + r), 0)
        for j in range(r, CONV_WIDTH, SUBLANES):
            acc = acc + shifted[j - r:j - r + tile] * convw_ref[0, j:j + 1, :]
    cv = _ln(acc + convb_ref[0]) * clng_ref[0] + clnb_ref[0]
    cv = cv * jax.nn.sigmoid(cv)
    ob = _dot(cv.astype(BF16), cpw_ref[0])

    pc = pcm_ref[0]
    win = jnp.concatenate([jnp.where(first, 0.0, pcp_ref[0]), pc, jnp.where(last, 0.0, pcn_ref[0])], axis=0)
    win = win + pltpu.roll(win, 1, 0)
    sums = [win]
    for step in (1, 2, 4):
        win = pltpu.roll(win, step, 0) + pltpu.roll(win, full - step, 0)
        sums.append(win)
    lane = lax.broadcasted_iota(jnp.int32, (tile, GROUP_W), 1)
    group = lane // SUBHEAD_DIM
    ssum = sums[0][HALO:HALO + tile]
    for gi in range(1, len(POOL_WINDOWS)):
        ssum = jnp.where(group == gi, sums[gi][HALO:HALO + tile], ssum)
    half = jnp.left_shift(1, group)
    pos = i * tile + lax.broadcasted_iota(jnp.int32, (tile, GROUP_W), 0)
    cnt = jnp.minimum(pos + half, seq) - jnp.maximum(pos - half, 0)
    pooled = ssum / cnt.astype(F32) - pc
    oc = _dot(pooled.astype(BF16), poolw_ref[0]) * pscale_ref[0]

    v = dv_ref[0]
    avg = avg_ref[...]
    v_hi, v_lo = _split(v)
    dv = v - (_dot(v_hi, avg) + _dot(v_lo, avg))
    q_hi, q_lo = _split(dv * dv)
    var = _dot(q_hi, avg) + _dot(q_lo, avg)
    vn = (dv * lax.rsqrt(var + LN_EPS) * slng_ref[0] + slnb_ref[0]).astype(BF16)
    wst = wst_ref[0]
    lane_c = lax.broadcasted_iota(jnp.int32, (CHUNK, GROUP_W), 1) // SUBHEAD_DIM
    mixed = []
    for c in range(tile // CHUNK):
        res = _dot(wst, vn[c * CHUNK:(c + 1) * CHUNK])
        sel = res[0:CHUNK]
        for h in range(1, N_SUBHEADS):
            sel = jnp.where(lane_c == h, res[h * CHUNK:(h + 1) * CHUNK], sel)
        mixed.append(sel + sbias_ref[0])
    od = du_ref[0] * jnp.concatenate(mixed, axis=0)

    g = GROUP_W
    mo = (_dot(oa_ref[0], wout_ref[0, 0:g]) + _dot(ob.astype(BF16), wout_ref[0, g:2 * g])
          + _dot(oc.astype(BF16), wout_ref[0, 2 * g:3 * g]) + _dot(od.astype(BF16), wout_ref[0, 3 * g:4 * g])
          + bout_ref[0])
    x1 = _ln(DN_ALPHA * x_ref[0] + m[2:3] * mo) * ln1g_ref[0] + ln1b_ref[0]
    x1_ref[0] = x1

    h2 = _ln(x1) * (1.0 + m[4:5]) + m[3:4]
    h2_ref[0] = h2
    logits = _dot3(h2, rwh_ref[0], rwl_ref[0]) + rb_ref[0]
    eio = lax.broadcasted_iota(jnp.int32, (tile, N_EXPERTS), 1).astype(F32)
    kio = lax.broadcasted_iota(jnp.int32, (tile, TOP_K), 1)
    work = logits
    vals, hots = [], []
    for _ in range(TOP_K):
        mx = jnp.max(work, axis=1, keepdims=True)
        am = jnp.min(jnp.where(work == mx, eio, float(N_EXPERTS)), axis=1, keepdims=True)
        hot = eio == am
        vals.append(mx)
        hots.append(hot)
        work = jnp.where(hot, -jnp.inf, work)
    exps = [jnp.exp(vv - vals[0]) for vv in vals]
    denom = exps[0] + exps[1] + exps[2] + exps[3]
    onehot = jnp.zeros((tile, N_EXPERTS), F32)
    for hot in hots:
        onehot = onehot + hot.astype(F32)
    before = _dot(tri_ref[...], onehot.astype(BF16))
    cnt = jnp.sum(onehot, axis=0, keepdims=True).astype(jnp.int32)
    cnt8 = jnp.broadcast_to(jnp.right_shift(cnt + (SUBLANES - 1), 3), (SUBLANES, N_EXPERTS))
    seg = _dot(cnt8.astype(BF16), upper_ref[...])[0:1] * float(SUBLANES)
    where_to = seg + before
    pos = jnp.zeros((tile, TOP_K), jnp.int32)
    wts = jnp.zeros((tile, TOP_K), F32)
    for k in range(TOP_K):
        pk = jnp.sum(jnp.where(hots[k], where_to, 0.0), axis=1, keepdims=True).astype(jnp.int32)
        pos = jnp.where(kio == k, pk, pos)
        wts = jnp.where(kio == k, exps[k] / denom, wts)
    pos_ref[...] = pos
    wts_ref[...] = wts
    cnt_ref[0] = cnt


def _mix(x, z, oa, mod, p, layer, tile):
    batch, seq, _ = x.shape
    nt = seq // tile
    hb = tile // HALO
    nhb = seq // HALO

    def col(c):
        return pl.BlockSpec((1, tile, GROUP_W), lambda b, i: (b, i, c))

    def prev(c):
        return pl.BlockSpec((1, HALO, GROUP_W), lambda b, i: (b, jnp.maximum(i * hb - 1, 0), c))

    def nxt(c):
        return pl.BlockSpec((1, HALO, GROUP_W), lambda b, i: (b, jnp.minimum((i + 1) * hb, nhb - 1), c))

    def lay(*shape):
        return pl.BlockSpec((1,) + shape, lambda b, i: (layer,) + (0,) * len(shape))

    def const(*shape):
        return pl.BlockSpec(shape, lambda b, i: (0,) * len(shape))

    row = pl.BlockSpec((1, tile, D_MODEL), lambda b, i: (b, i, 0))
    sel = pl.BlockSpec((tile, TOP_K), lambda b, i: (b * nt + i, 0))
    in_specs = [
        row, col(0),
        prev(1), col(1), nxt(1), prev(2), col(2), nxt(2),
        prev(3), col(3), nxt(3), col(4), col(5),
        pl.BlockSpec((1, 6, D_MODEL), lambda b, i: (b, 0, 0)),
        lay(CONV_WIDTH, GROUP_W), lay(1, GROUP_W), lay(1, GROUP_W), lay(1, GROUP_W), lay(GROUP_W, GROUP_W),
        lay(GROUP_W, GROUP_W), lay(1, GROUP_W),
        lay(1, GROUP_W), lay(1, GROUP_W), lay(N_SUBHEADS * CHUNK, CHUNK), lay(CHUNK, GROUP_W),
        const(GROUP_W, GROUP_W),
        lay(D_MODEL, D_MODEL), lay(1, D_MODEL), lay(1, D_MODEL), lay(1, D_MODEL),
        lay(D_MODEL, N_EXPERTS), lay(D_MODEL, N_EXPERTS), lay(1, N_EXPERTS),
        const(tile, tile), const(N_EXPERTS, N_EXPERTS),
    ]
    out_specs = [row, row, sel, sel,
                 pl.BlockSpec((1, 1, N_EXPERTS), lambda b, i: (b * nt + i, 0, 0))]
    out_shape = [jax.ShapeDtypeStruct((batch, seq, D_MODEL), F32),
                 jax.ShapeDtypeStruct((batch, seq, D_MODEL), F32),
                 jax.ShapeDtypeStruct((batch * seq, TOP_K), jnp.int32),
                 jax.ShapeDtypeStruct((batch * seq, TOP_K), F32),
                 jax.ShapeDtypeStruct((batch * nt, 1, N_EXPERTS), jnp.int32)]
    tri = jnp.asarray(np.tril(np.ones((tile, tile), np.float32), -1), BF16)
    upper = jnp.asarray(np.triu(np.ones((N_EXPERTS, N_EXPERTS), np.float32), 1), BF16)
    return pl.pallas_call(
        functools.partial(_mix_kernel, seq, tile),
        grid=(batch, nt),
        in_specs=in_specs,
        out_specs=out_specs,
        out_shape=out_shape,
        compiler_params=_params("arbitrary", "arbitrary"),
        name="mix",
    )(x, oa, z, z, z, z, z, z, z, z, z, z, z, mod,
      p["conv_w"], p["conv_b"], p["conv_ln_g"], p["conv_ln_b"], p["conv_pw"],
      p["pool_w"], p["pool_scale"],
      p["sgu_ln_g"], p["sgu_ln_b"], p["sgu_w"], p["sgu_b"], p["avg"],
      p["w_out"], p["b_out"], p["ln1_g"], p["ln1_b"],
      p["router_hi"], p["router_lo"], p["router_b"], tri, upper)


def _sorted_rows(tile):
    return TOP_K * tile + N_EXPERTS * SUBLANES


def _route(cnt, tile, rows):
    cnt = cnt.reshape(-1, N_EXPERTS)
    n_tiles = cnt.shape[0]
    r8 = rows // SUBLANES
    cnt8 = (cnt + SUBLANES - 1) // SUBLANES
    seg8 = jnp.cumsum(cnt8, axis=1) - cnt8
    tot8 = jnp.sum(cnt8, axis=0)
    pad8 = ((tot8 + r8 - 1) // r8) * r8
    cum8 = jnp.cumsum(pad8)
    start8 = cum8 - pad8
    gb8 = start8[None, :] + jnp.cumsum(cnt8, axis=0) - cnt8
    n_blocks = -(-(n_tiles * tile * TOP_K + (SUBLANES - 1) * N_EXPERTS * n_tiles) // rows) + N_EXPERTS
    block_start8 = jnp.arange(n_blocks, dtype=cum8.dtype) * r8
    block_e = jnp.minimum(jnp.sum(cum8[None, :] <= block_start8[:, None], axis=1), N_EXPERTS - 1)
    used = (cum8[-1:] // r8)
    tile_units = jnp.broadcast_to(jnp.sum(cnt8, axis=1, keepdims=True), cnt8.shape)
    segs = jnp.stack([seg8, cnt8, gb8, tile_units], axis=1).reshape(-1)
    tails = jnp.concatenate([start8 + tot8, pad8 - tot8, used, jnp.zeros((2 * N_EXPERTS - 1,), tot8.dtype)])
    i32 = lambda t: t.astype(jnp.int32)
    return i32(segs), i32(tails), i32(block_e), i32(used), n_blocks


def _segment_chunks(units, bits, visit):
    for b in range(bits):
        @pl.when(((units >> b) & 1) == 1)
        def _():
            visit(((units >> (b + 1)) << (b + 1)) * SUBLANES, SUBLANES << b)


def _start_segment_copies(seg_ref, tile_buf, sorted_hbm, sem, to_hbm):
    low_bits = (BIG_ROWS // SUBLANES).bit_length() - 1

    def body(e, carry):
        seg = seg_ref[e] * SUBLANES
        glob = seg_ref[2 * N_EXPERTS + e] * SUBLANES
        units = seg_ref[N_EXPERTS + e]

        def start(off, size):
            a = tile_buf.at[pl.ds(pl.multiple_of(seg + off, SUBLANES), size)]
            b = sorted_hbm.at[pl.ds(pl.multiple_of(glob + off, SUBLANES), size)]
            (pltpu.make_async_copy(a, b, sem) if to_hbm else pltpu.make_async_copy(b, a, sem)).start()

        n_big = units >> low_bits

        def big(c, carry):
            start(c * BIG_ROWS, BIG_ROWS)
            return carry

        lax.fori_loop(0, n_big, big, 0)
        _segment_chunks(units & ((1 << low_bits) - 1), low_bits, lambda off, size: start(n_big * BIG_ROWS + off, size))
        return carry

    lax.fori_loop(0, N_EXPERTS, body, 0)


def _wait_rows(units, bits, hbm, sem):
    def visit(off, size):
        del off
        pltpu.make_async_copy(hbm.at[pl.ds(0, size)], hbm.at[pl.ds(0, size)], sem).wait()

    _segment_chunks(units, bits, visit)


def _sorted_chunks(q):
    n = next(n for n in (3, 2, 1) if (q // 128) % n == 0)
    return [(c * (q // n), q // n) for c in range(n)]


def _dispatch_kernel(tile, seg_bits, tail_bits, seg_ref, prev_ref, tail_ref, pos_ref, h2_ref, xs_hbm,
                     xbuf, zbuf, sems):
    i = pl.program_id(0)
    slot = i % 2
    q = xbuf.shape[1]
    pos = pos_ref[...]
    lane = lax.broadcasted_iota(jnp.int32, (tile, 128), 1)
    canvas = jnp.full((tile, 128), -1.0, F32)
    for k in range(TOP_K):
        canvas = jnp.where(lane == k, pos[:, k:k + 1].astype(F32), canvas)
    pos_t = canvas.T.astype(jnp.int32)
    h2b = h2_ref[...].astype(BF16)
    for start, size in _sorted_chunks(q):
        qi = start + lax.broadcasted_iota(jnp.int32, (size, tile), 0)
        hit = qi == pos_t[0:1]
        for k in range(1, TOP_K):
            hit = hit | (qi == pos_t[k:k + 1])
        xbuf[slot, start:start + size] = _dot(jnp.where(hit, 1.0, 0.0).astype(BF16), h2b)

    @pl.when(i > 0)
    def _():
        _wait_rows(prev_ref[3 * N_EXPERTS], seg_bits, xs_hbm, sems.at[1 - slot])

    _start_segment_copies(seg_ref, xbuf.at[slot], xs_hbm, sems.at[slot], True)

    @pl.when(i == 0)
    def _():
        zbuf[...] = jnp.zeros(zbuf.shape, F32)

    @pl.when(i == pl.num_programs(0) - 1)
    def _():
        _wait_rows(seg_ref[3 * N_EXPERTS], seg_bits, xs_hbm, sems.at[slot])
        sem = sems.at[0]

        def tails(action):
            def body(e, carry):
                def visit(off, size):
                    dst = xs_hbm.at[pl.ds(pl.multiple_of(tail_ref[e] * SUBLANES + off, SUBLANES), size)]
                    action(pltpu.make_async_copy(zbuf.at[pl.ds(0, size)], dst, sem))

                _segment_chunks(tail_ref[N_EXPERTS + e], tail_bits, visit)
                return carry

            lax.fori_loop(0, N_EXPERTS, body, 0)

        tails(lambda cp: cp.start())
        tails(lambda cp: cp.wait())

        rows = zbuf.shape[0]

        def fill(g, carry):
            cp = pltpu.make_async_copy(zbuf, xs_hbm.at[pl.ds(pl.multiple_of(g * rows, rows), rows)], sem)
            cp.start()
            cp.wait()
            return carry

        lax.fori_loop(tail_ref[2 * N_EXPERTS], xs_hbm.shape[0] // rows, fill, 0)


def _dispatch(segs, tails, pos, h2, total_rows, tile, rows):
    n = h2.shape[0]
    seg_bits = (_sorted_rows(tile) // SUBLANES).bit_length()
    tail_bits = (rows // SUBLANES - 1).bit_length()
    return pl.pallas_call(
        functools.partial(_dispatch_kernel, tile, seg_bits, tail_bits),
        grid=(n // tile,),
        in_specs=[pl.BlockSpec((4 * N_EXPERTS,), lambda i: (i,), memory_space=pltpu.SMEM),
                  pl.BlockSpec((4 * N_EXPERTS,), lambda i: (jnp.maximum(i - 1, 0),), memory_space=pltpu.SMEM),
                  pl.BlockSpec((4 * N_EXPERTS,), lambda i: (0,), memory_space=pltpu.SMEM),
                  pl.BlockSpec((tile, TOP_K), lambda i: (i, 0)),
                  pl.BlockSpec((tile, D_MODEL), lambda i: (i, 0))],
        out_specs=pl.BlockSpec(memory_space=pl.ANY),
        out_shape=jax.ShapeDtypeStruct((total_rows, D_MODEL), F32),
        scratch_shapes=[pltpu.VMEM((2, _sorted_rows(tile), D_MODEL), F32),
                        pltpu.VMEM((rows, D_MODEL), F32),
                        pltpu.SemaphoreType.DMA((2,))],
        compiler_params=_params("arbitrary"),
        name="dispatch",
    )(segs, segs, tails, pos, h2)


def _gu_prep_kernel(w_ref, perm_ref, o_ref):
    o_ref[0, 0] = _dot(w_ref[0, 0].astype(BF16), perm_ref[...]).astype(BF16)


def _gu_prep(w_gu):
    nl, ne, d, c = w_gu.shape
    perm = np.zeros((c, c), np.float32)
    perm[_GU_ORDER, np.arange(c)] = 1.0
    blk = pl.BlockSpec((1, 1, d, c), lambda l, e: (l, e, 0, 0))
    return pl.pallas_call(
        _gu_prep_kernel,
        grid=(nl, ne),
        in_specs=[blk, pl.BlockSpec((c, c), lambda l, e: (0, 0))],
        out_specs=blk,
        out_shape=jax.ShapeDtypeStruct(w_gu.shape, BF16),
        compiler_params=_params("arbitrary", "arbitrary"),
        name="gu_prep",
    )(w_gu, jnp.asarray(perm, BF16))


def _expert_kernel(be_ref, used_ref, xs_ref, wgu_ref, bgu_ref, wd_ref, bd_ref, y_ref):
    del be_ref

    @pl.when(pl.program_id(0) >= used_ref[0])
    def _():
        y_ref[...] = jnp.zeros(y_ref.shape, F32)

    @pl.when(pl.program_id(0) < used_ref[0])
    def _():
        hu = _dot(xs_ref[...].astype(BF16), wgu_ref[0, 0]) + bgu_ref[0, 0]
        glu = jnp.minimum(hu[:, :D_EXPERT], SWIGLU_LIMIT)
        lin = jnp.clip(hu[:, D_EXPERT:], -SWIGLU_LIMIT, SWIGLU_LIMIT)
        act = glu * jax.nn.sigmoid(SWIGLU_ALPHA * glu) * (lin + 1.0)
        y_ref[...] = _dot(act.astype(BF16), wd_ref[0, 0]) + bd_ref[0, 0]


def _experts(block_e, used, xs, p, layer, rows, n_blocks):
    def wspec(*shape):
        return pl.BlockSpec((1, 1) + shape, lambda g, be, u: (layer, be[g]) + (0,) * len(shape))

    grid_spec = pltpu.PrefetchScalarGridSpec(
        num_scalar_prefetch=2,
        grid=(n_blocks,),
        in_specs=[pl.BlockSpec((rows, D_MODEL), lambda g, be, u: (g, 0)),
                  wspec(D_MODEL, 2 * D_EXPERT), wspec(1, 2 * D_EXPERT),
                  wspec(D_EXPERT, D_MODEL), wspec(1, D_MODEL)],
        out_specs=pl.BlockSpec((rows, D_MODEL), lambda g, be, u: (g, 0)),
    )
    return pl.pallas_call(
        _expert_kernel,
        grid_spec=grid_spec,
        out_shape=jax.ShapeDtypeStruct(xs.shape, F32),
        compiler_params=_params("arbitrary"),
        name="experts",
    )(block_e, used, xs, p["w_gu"], p["b_gu"], p["w_dn"], p["b_dn"])


def _combine_kernel(tile, seg_bits, seg_ref, next_ref, y_hbm, pos_ref, x1_ref, wts_ref, mod_ref, g_ref, b_ref,
                    o_ref, ybuf, sems):
    t = pl.program_id(0) * pl.num_programs(1) + pl.program_id(1)
    n_tiles = pl.num_programs(0) * pl.num_programs(1)
    slot = t % 2

    @pl.when(t == 0)
    def _():
        ybuf[...] = jnp.zeros(ybuf.shape, F32)
        _start_segment_copies(seg_ref, ybuf.at[0], y_hbm, sems.at[0], False)

    @pl.when(t + 1 < n_tiles)
    def _():
        _start_segment_copies(next_ref, ybuf.at[1 - slot], y_hbm, sems.at[1 - slot], False)

    _wait_rows(seg_ref[3 * N_EXPERTS], seg_bits, y_hbm, sems.at[slot])
    pos = pos_ref[...]
    wts = wts_ref[...]
    f = jnp.zeros((tile, D_MODEL), F32)
    for start, size in _sorted_chunks(ybuf.shape[1]):
        qi = start + lax.broadcasted_iota(jnp.int32, (tile, size), 1)
        wm = jnp.zeros((tile, size), F32)
        for k in range(TOP_K):
            wm = jnp.where(qi == pos[:, k:k + 1], wts[:, k:k + 1], wm)
        f = f + _dot(wm.astype(BF16), ybuf[slot, start:start + size].astype(BF16))
    m = mod_ref[0]
    o_ref[0] = _ln(DN_ALPHA * x1_ref[0] + m[5:6] * f) * g_ref[0] + b_ref[0]


def _combine(segs, pos, y, x1, wts, mod, p, layer, tile):
    batch, seq, _ = x1.shape
    nt = seq // tile
    last = batch * nt - 1
    row = pl.BlockSpec((1, tile, D_MODEL), lambda b, i: (b, i, 0))
    sel = pl.BlockSpec((tile, TOP_K), lambda b, i: (b * nt + i, 0))
    vec = pl.BlockSpec((1, 1, D_MODEL), lambda b, i: (layer, 0, 0))
    return pl.pallas_call(
        functools.partial(_combine_kernel, tile, (_sorted_rows(tile) // SUBLANES).bit_length()),
        grid=(batch, nt),
        in_specs=[pl.BlockSpec((4 * N_EXPERTS,), lambda b, i: (b * nt + i,), memory_space=pltpu.SMEM),
                  pl.BlockSpec((4 * N_EXPERTS,), lambda b, i: (jnp.minimum(b * nt + i + 1, last),),
                               memory_space=pltpu.SMEM),
                  pl.BlockSpec(memory_space=pl.ANY),
                  sel, row, sel,
                  pl.BlockSpec((1, 6, D_MODEL), lambda b, i: (b, 0, 0)),
                  vec, vec],
        out_specs=row,
        out_shape=jax.ShapeDtypeStruct(x1.shape, F32),
        scratch_shapes=[pltpu.VMEM((2, _sorted_rows(tile), D_MODEL), F32), pltpu.SemaphoreType.DMA((2,))],
        compiler_params=_params("arbitrary", "arbitrary"),
        name="combine",
    )(segs, segs, y, pos, x1, wts, mod, p["ln2_g"], p["ln2_b"])


def _prepare(w_in, b_in, fnet_w, conv_w, conv_b, conv_ln_g, conv_ln_b, conv_pw, pool_w, pool_scale,
             sgu_ln_g, sgu_ln_b, sgu_w, sgu_b, w_out, b_out, ln1_g, ln1_b, router_w, router_b,
             w_gu, b_gu, w_dn, b_dn, ln2_g, ln2_b):
    nl = w_in.shape[0]
    vec = lambda t: t.reshape(nl, 1, -1)
    group = np.arange(GROUP_W) // SUBHEAD_DIM
    avg = (group[:, None] == group[None, :]).astype(np.float32) / SUBHEAD_DIM
    r_hi = router_w.astype(BF16)
    m1, m2 = _fnet_prep(fnet_w)
    return {
        "w_in": w_in.astype(BF16), "b_in": vec(b_in), "fnet_m1": m1, "fnet_m2": m2,
        "conv_w": conv_w, "conv_b": vec(conv_b), "conv_ln_g": vec(conv_ln_g), "conv_ln_b": vec(conv_ln_b),
        "conv_pw": conv_pw.astype(BF16),
        "pool_w": _block_diag(pool_w).astype(BF16),
        "pool_scale": vec(pool_scale),
        "sgu_ln_g": vec(sgu_ln_g), "sgu_ln_b": vec(sgu_ln_b),
        "sgu_w": sgu_w.reshape(nl, N_SUBHEADS * CHUNK, CHUNK).astype(BF16),
        "sgu_b": jnp.repeat(jnp.swapaxes(sgu_b, 1, 2), SUBHEAD_DIM, axis=2),
        "avg": jnp.asarray(avg, BF16),
        "w_out": w_out.astype(BF16), "b_out": vec(b_out), "ln1_g": vec(ln1_g), "ln1_b": vec(ln1_b),
        "router_hi": r_hi, "router_lo": (router_w - r_hi.astype(F32)).astype(BF16), "router_b": vec(router_b),
        "w_gu": _gu_prep(w_gu), "b_gu": b_gu[..., None, _GU_ORDER],
        "w_dn": w_dn.astype(BF16), "b_dn": b_dn[..., None, :],
        "ln2_g": vec(ln2_g), "ln2_b": vec(ln2_b),
    }


def _trunk(x, mod, p):
    batch, seq, _ = x.shape
    tile, rows = _tiles(seq)
    tables = _fft_tables(seq)
    for layer in range(DEPTH):
        z = _inproj(x, mod[layer], p["w_in"], p["b_in"], layer, tile)
        oa = _fourier(z, batch, seq, tables, p["fnet_m1"], p["fnet_m2"], layer)
        x1, h2, pos, wts, cnt = _mix(x, z, oa, mod[layer], p, layer, tile)
        segs, tails, block_e, used, n_blocks = _route(cnt, tile, rows)
        xs = _dispatch(segs, tails, pos, h2.reshape(batch * seq, D_MODEL), n_blocks * rows, tile, rows)
        y = _experts(block_e, used, xs, p, layer, rows, n_blocks)
        x = _combine(segs, pos, y, x1, wts, mod[layer], p, layer, tile)
    return x


def kernel(x_prompt, x_sample, c_prompt, c_sample, w_mod, b_mod, w_in, b_in, fnet_w, conv_w, conv_b,
           conv_ln_g, conv_ln_b, conv_pw, pool_w, pool_scale, sgu_ln_g, sgu_ln_b, sgu_w, sgu_b,
           w_out, b_out, ln1_g, ln1_b, router_w, router_b, w_gu, b_gu, w_dn, b_dn, ln2_g, ln2_b):
    p = _prepare(w_in, b_in, fnet_w, conv_w, conv_b, conv_ln_g, conv_ln_b, conv_pw, pool_w, pool_scale,
                 sgu_ln_g, sgu_ln_b, sgu_w, sgu_b, w_out, b_out, ln1_g, ln1_b, router_w, router_b,
                 w_gu, b_gu, w_dn, b_dn, ln2_g, ln2_b)
    nb_p, nb_s = c_prompt.shape[0], c_sample.shape[0]
    pad = (-(nb_p + nb_s)) % 8
    c_all = jnp.concatenate([c_prompt, c_sample, jnp.zeros((pad, D_MODEL), F32)], axis=0)
    mod = _modulation(c_all, w_mod, b_mod)
    nl = mod.shape[0]
    mod_p = mod[:, :nb_p].reshape(nl, nb_p, 6, D_MODEL)
    mod_s = mod[:, nb_p:nb_p + nb_s].reshape(nl, nb_s, 6, D_MODEL)
    return _trunk(x_prompt, mod_p, p), _trunk(x_sample, mod_s, p)
```
